```python
import jax, jax.numpy as jnp
from jax import lax
import numpy as np

D_MODEL = 2048
BATCH = 4
SEQ = 8192
DEPTH = 1

PLE_DIM = 256
HEAD_DIM = 128
ATTN_HEADS = 8
RET_HEADS = 8
ATTN_WIDTH = ATTN_HEADS * HEAD_DIM
RET_WIDTH = RET_HEADS * HEAD_DIM
MIX_WIDTH = ATTN_WIDTH + RET_WIDTH
IN_COLS = 3 * ATTN_WIDTH + 4 * RET_WIDTH
D_FF = -(-(8 * D_MODEL) // (3 * 256)) * 256
MOBA_BLOCK = 256
MOBA_TOPK = 3
MOBA_Q_CHUNK = 32
RET_CHUNK = 128
ROPE_THETA = 10000.0
RMS_EPS = 1e-6
NEG_INF = -1e30

kernel_name = "hymba_moba_retnet_hybrid_layer"


def rms_norm(x, g):
    xf = x.astype(jnp.float32)
    y = xf * lax.rsqrt(jnp.mean(xf * xf, axis=-1, keepdims=True) + RMS_EPS)
    return (y * g.astype(jnp.float32)).astype(x.dtype)


def rotate_half_split(x, pos, inv_freq):
    ang = pos[:, None] * inv_freq[None, :]
    cos = jnp.cos(ang).astype(x.dtype)
    sin = jnp.sin(ang).astype(x.dtype)
    x1, x2 = jnp.split(x, 2, axis=-1)
    return jnp.concatenate([x1 * cos - x2 * sin, x2 * cos + x1 * sin], axis=-1)


def moba_attention(q, k, v):
    B, H, T, d = q.shape
    BS = MOBA_BLOCK
    QC = MOBA_Q_CHUNK
    NB = -(-T // BS)
    Tp = NB * BS
    NC = T // QC
    K_SEL = min(MOBA_TOPK, max(NB - 1, 1))
    scale = 1.0 / np.sqrt(d)

    pad = ((0, 0), (0, 0), (0, Tp - T), (0, 0))
    kb = jnp.pad(k, pad).reshape(B, H, NB, BS, d)
    vb = jnp.pad(v, pad).reshape(B, H, NB, BS, d)

    k_mean = jnp.mean(kb.astype(jnp.float32), axis=3)
    gate = jnp.einsum('bhtd,bhnd->bhtn', q.astype(jnp.float32), k_mean)
    q_block = jnp.arange(T) // BS
    past = jnp.arange(NB)[None, :] < q_block[:, None]
    gate = jnp.where(past[None, None], gate, NEG_INF)
    _, sel_idx = lax.top_k(gate, K_SEL)

    b_idx = jnp.arange(B)[:, None, None, None]
    h_idx = jnp.arange(H)[None, :, None, None]

    def chunk_step(args):
        c, q_c, idx_c = args
        t = c * QC + jnp.arange(QC)
        kg = kb[b_idx, h_idx, idx_c].reshape(B, H, QC, K_SEL * BS, d)
        vg = vb[b_idx, h_idx, idx_c].reshape(B, H, QC, K_SEL * BS, d)
        valid = jnp.arange(K_SEL)[None, :] < (t // BS)[:, None]
        valid = jnp.repeat(valid, BS, axis=1)
        s_sel = jnp.einsum('bhqd,bhqnd->bhqn', q_c, kg).astype(jnp.float32) * scale
        s_sel = jnp.where(valid[None, None], s_sel, NEG_INF)
        b0 = (c * QC) // BS
        k_own = lax.dynamic_index_in_dim(kb, b0, axis=2, keepdims=False)
        v_own = lax.dynamic_index_in_dim(vb, b0, axis=2, keepdims=False)
        key_pos = b0 * BS + jnp.arange(BS)
        causal = key_pos[None, :] <= t[:, None]
        s_own = jnp.einsum('bhqd,bhkd->bhqk', q_c, k_own).astype(jnp.float32) * scale
        s_own = jnp.where(causal[None, None], s_own, NEG_INF)
        probs = jax.nn.softmax(jnp.concatenate([s_sel, s_own], axis=-1), axis=-1)
        p_sel = probs[..., :K_SEL * BS].astype(v.dtype)
        p_own = probs[..., K_SEL * BS:].astype(v.dtype)
        return (jnp.einsum('bhqn,bhqnd->bhqd', p_sel, vg)
                + jnp.einsum('bhqk,bhkd->bhqd', p_own, v_own))

    q_ch = jnp.moveaxis(q.reshape(B, H, NC, QC, d), 2, 0)
    idx_ch = jnp.moveaxis(sel_idx.reshape(B, H, NC, QC, K_SEL), 2, 0)
    out = lax.map(chunk_step, (jnp.arange(NC), q_ch, idx_ch))
    return jnp.moveaxis(out, 0, 2).reshape(B, H, T, d)


def retention_chunkwise(q, k, v):
    B, H, T, d = q.shape
    dv = v.shape[-1]
    C = RET_CHUNK
    N = T // C
    log_g = jnp.log(1.0 - jnp.power(2.0, -5.0 - jnp.arange(H, dtype=jnp.float32)))
    j = jnp.arange(C, dtype=jnp.float32)
    rel = j[:, None] - j[None, :]
    decay = jnp.where(rel >= 0, jnp.exp(log_g[:, None, None] * jnp.maximum(rel, 0.0)), 0.0)

    qc = q.reshape(B, H, N, C, d)
    kc = k.reshape(B, H, N, C, d)
    vc = v.reshape(B, H, N, C, dv)

    s = jnp.einsum('bhncd,bhnmd->bhncm', qc, kc) * decay[None, :, None]
    intra = jnp.einsum('bhncm,bhnme->bhnce', s, vc)

    k_dec = kc * jnp.exp(log_g[:, None] * (C - 1.0 - j)[None, :])[None, :, None, :, None]
    U = jnp.einsum('bhncd,bhnce->nbhde', k_dec, vc)
    g_chunk = jnp.exp(log_g * C)[None, :, None, None]

    def step(S, U_n):
        return g_chunk * S + U_n, S

    _, S_prev = lax.scan(step, jnp.zeros((B, H, d, dv), jnp.float32), U)
    q_dec = qc * jnp.exp(log_g[:, None] * (j + 1.0)[None, :])[None, :, None, :, None]
    cross = jnp.einsum('bhncd,nbhde->bhnce', q_dec, S_prev)
    return (intra + cross).reshape(B, H, T, dv)


def setup_inputs(seed: int = 0) -> dict:
    key = jax.random.key(seed)
    ks = jax.random.split(key, 20)
    f32 = jnp.float32
    nrm = lambda k, shape, fan: jax.random.normal(k, shape, f32) * (fan ** -0.5)
    gain = lambda k, shape: 1.0 + 0.05 * jax.random.normal(k, shape, f32)
    return {
        "x": jax.random.normal(ks[0], (BATCH, SEQ, D_MODEL), f32),
        "p": jax.random.normal(ks[1], (DEPTH, BATCH, SEQ, PLE_DIM), f32),
        "g_mix": gain(ks[2], (DEPTH, D_MODEL)),
        "w_in": nrm(ks[3], (DEPTH, D_MODEL, IN_COLS), D_MODEL),
        "q_norm": gain(ks[4], (DEPTH, HEAD_DIM)),
        "k_norm": gain(ks[5], (DEPTH, HEAD_DIM)),
        "g_ret": gain(ks[6], (DEPTH, RET_WIDTH)),
        "w_o": nrm(ks[7], (DEPTH, MIX_WIDTH, D_MODEL), MIX_WIDTH),
        "g_ffn": gain(ks[8], (DEPTH, D_MODEL)),
        "w_gate": nrm(ks[9], (DEPTH, D_MODEL, D_FF), D_MODEL),
        "w_up": nrm(ks[10], (DEPTH, D_MODEL, D_FF), D_MODEL),
        "w_down": nrm(ks[11], (DEPTH, D_FF, D_MODEL), D_FF),
        "g_ple": gain(ks[12], (DEPTH, D_MODEL)),
        "w_ple_gate": nrm(ks[13], (DEPTH, D_MODEL, D_MODEL), D_MODEL),
        "b_ple_gate": 0.01 * jax.random.normal(ks[14], (DEPTH, D_MODEL), f32),
        "w_ple_proj": nrm(ks[15], (DEPTH, PLE_DIM, D_MODEL), PLE_DIM),
    }


def reference(x, p, g_mix, w_in, q_norm, k_norm, g_ret, w_o, g_ffn, w_gate, w_up,
              w_down, g_ple, w_ple_gate, b_ple_gate, w_ple_proj):
    B, T, _ = x.shape
    pos = jnp.arange(T, dtype=jnp.float32)
    inv_freq_attn = jnp.power(ROPE_THETA, -jnp.arange(0, HEAD_DIM, 2, dtype=jnp.float32) / HEAD_DIM)
    inv_freq_ret = jnp.power(ROPE_THETA, -jnp.linspace(0.0, 1.0, HEAD_DIM // 2, dtype=jnp.float32))
    split_at = np.cumsum([ATTN_WIDTH] * 3 + [RET_WIDTH] * 3).tolist()

    def heads(t, n):
        return t.reshape(B, T, n, HEAD_DIM).transpose(0, 2, 1, 3)

    for i in range(DEPTH):
        h = rms_norm(x, g_mix[i])
        z = h @ w_in[i]
        aq, ak, av, rq, rk, rv, rg = jnp.split(z, split_at, axis=-1)

        aq = rotate_half_split(heads(rms_norm(aq.reshape(B, T, ATTN_HEADS, HEAD_DIM), q_norm[i]).reshape(B, T, ATTN_WIDTH), ATTN_HEADS), pos, inv_freq_attn)
        ak = rotate_half_split(heads(rms_norm(ak.reshape(B, T, ATTN_HEADS, HEAD_DIM), k_norm[i]).reshape(B, T, ATTN_WIDTH), ATTN_HEADS), pos, inv_freq_attn)
        a_out = moba_attention(aq, ak, heads(av, ATTN_HEADS))
        a_out = a_out.transpose(0, 2, 1, 3).reshape(B, T, ATTN_WIDTH)

        rq_h = rotate_half_split(heads(rq, RET_HEADS), pos, inv_freq_ret).astype(jnp.float32)
        rk_h = (rotate_half_split(heads(rk, RET_HEADS), pos, inv_freq_ret).astype(jnp.float32)
                * (HEAD_DIM ** -0.5))
        r = retention_chunkwise(rq_h, rk_h, heads(rv, RET_HEADS).astype(jnp.float32))
        r = r * lax.rsqrt(jnp.mean(r * r, axis=-1, keepdims=True) + RMS_EPS)
        r = r.transpose(0, 2, 1, 3).reshape(B, T, RET_WIDTH) * g_ret[i].astype(jnp.float32)
        r_out = (jax.nn.silu(rg.astype(jnp.float32)) * r).astype(x.dtype)

        x = x + jnp.concatenate([a_out, r_out], axis=-1) @ w_o[i]

        h2 = rms_norm(x, g_ffn[i])
        x = x + (jax.nn.silu(h2 @ w_gate[i]) * (h2 @ w_up[i])) @ w_down[i]

        gate = jax.nn.sigmoid(rms_norm(x, g_ple[i]) @ w_ple_gate[i] + b_ple_gate[i])
        x = x + (p[i] @ w_ple_proj[i]) * gate
    return x
```

```python
import functools

import jax
import jax.numpy as jnp
from jax import lax
from jax.experimental import pallas as pl
from jax.experimental.pallas import tpu as pltpu

HEAD_DIM = 128
ATTN_HEADS = 8
RET_HEADS = 8
ATTN_WIDTH = ATTN_HEADS * HEAD_DIM
RET_WIDTH = RET_HEADS * HEAD_DIM
MOBA_BLOCK = 256
MOBA_TOPK = 3
ROPE_THETA = 10000.0
RMS_EPS = 1e-6
NEG_INF = -1e30

N_SEGMENTS = 7
SEG_AQ, SEG_AK, SEG_AV, SEG_RQ, SEG_RK, SEG_RV, SEG_RG = range(N_SEGMENTS)
ROPE_ATTN, ROPE_RET, ROPE_NONE = 0, 1, 2

LANES = 128
VMEM_LIMIT_BYTES = 56 * 1024 * 1024

F32 = jnp.float32
BF16 = jnp.bfloat16

_NT = (((1,), (1,)), ((), ()))
_TN = (((0,), (0,)), ((), ()))


def _sigmoid(x):
    return 1.0 / (1.0 + jnp.exp(-x))


def _rms_scale(x):
    return lax.rsqrt(jnp.mean(x * x, axis=-1, keepdims=True) + RMS_EPS)


def _pick(dim, pref):
    t = min(dim, pref)
    while dim % t:
        t //= 2
    return t


def _inproj_kernel(x_ref, gmix_ref, w_ref, gain_ref, cos_ref, sin_ref, z_ref, h_scr, *,
                   norm_col_blocks, heads_per_block):
    j = pl.program_id(1)

    @pl.when(j == 0)
    def _():
        x = x_ref[...]
        h_scr[...] = (x * _rms_scale(x) * gmix_ref[...]).astype(h_scr.dtype)

    acc = jnp.dot(h_scr[...], w_ref[...], preferred_element_type=F32)
    is_norm = j < norm_col_blocks
    gain = gain_ref[0]
    cos = cos_ref[0]
    sin = sin_ref[0]
    for hh in range(heads_per_block):
        y = acc[:, hh * HEAD_DIM:(hh + 1) * HEAD_DIM]
        nrm = jnp.where(is_norm, _rms_scale(y), 1.0)
        y = y * nrm * gain
        z_ref[hh] = (y * cos + pltpu.roll(y, HEAD_DIM // 2, 1) * sin).astype(z_ref.dtype)


def _in_projection(x, g_mix, w_in, gains, cos_tab, sin_tab, seq_len):
    M, D = x.shape
    n_cols = w_in.shape[1]
    tm = _pick(seq_len, 1024)
    tn = _pick(ATTN_WIDTH, 512)
    blocks_per_seg = ATTN_WIDTH // tn
    t_blocks = seq_len // tm

    def seg_of(j):
        return j // blocks_per_seg

    def rope_of(j):
        s = seg_of(j)
        return jnp.where(s <= SEG_AK, ROPE_ATTN,
                         jnp.where((s == SEG_RQ) | (s == SEG_RK), ROPE_RET, ROPE_NONE))

    kern = functools.partial(_inproj_kernel, norm_col_blocks=2 * blocks_per_seg,
                             heads_per_block=tn // HEAD_DIM)
    return pl.pallas_call(
        kern,
        grid=(M // tm, n_cols // tn),
        in_specs=[
            pl.BlockSpec((tm, D), lambda i, j: (i, 0)),
            pl.BlockSpec((1, D), lambda i, j: (0, 0)),
            pl.BlockSpec((D, tn), lambda i, j: (0, j)),
            pl.BlockSpec((1, 1, HEAD_DIM), lambda i, j: (seg_of(j), 0, 0)),
            pl.BlockSpec((1, tm, HEAD_DIM), lambda i, j: (rope_of(j), i % t_blocks, 0)),
            pl.BlockSpec((1, tm, HEAD_DIM), lambda i, j: (rope_of(j), i % t_blocks, 0)),
        ],
        out_specs=pl.BlockSpec((tn // HEAD_DIM, tm, HEAD_DIM), lambda i, j: (j, i, 0)),
        out_shape=jax.ShapeDtypeStruct((n_cols // HEAD_DIM, M, HEAD_DIM), BF16),
        scratch_shapes=[pltpu.VMEM((tm, D), BF16)],
        compiler_params=pltpu.CompilerParams(
            dimension_semantics=("parallel", "arbitrary"), vmem_limit_bytes=VMEM_LIMIT_BYTES),
        name="in_projection",
    )(x, g_mix, w_in, gains, cos_tab, sin_tab)


def _moba_kernel(q_ref, k_ref, v_ref, o_ref, kmean_scr, *, n_blocks, scale):
    BS = MOBA_BLOCK
    qi = pl.program_id(1)

    @pl.when(qi == 0)
    def _():
        for n in range(n_blocks):
            kb = k_ref[0, n * BS:(n + 1) * BS, :].astype(F32)
            kmean_scr[n:n + 1, :] = jnp.mean(kb, axis=0, keepdims=True)

    q = q_ref[0]

    gate = lax.dot_general(kmean_scr[...].astype(BF16), q, _NT, preferred_element_type=F32)
    blk = lax.broadcasted_iota(jnp.int32, gate.shape, 0)
    blk_f = blk.astype(F32)
    gate = jnp.where(blk < qi, gate, NEG_INF)
    sel = []
    for r in range(MOBA_TOPK):
        mx = jnp.max(gate, axis=0, keepdims=True)
        pick = jnp.min(jnp.where(gate == mx, blk_f, float(n_blocks)), axis=0, keepdims=True)
        sel.append(jnp.where(r < qi, pick, -1.0))
        gate = jnp.where(blk_f == pick, -jnp.inf, gate)

    def tile(start, bias, state):
        kn = k_ref[0, pl.ds(start, BS), :]
        vn = v_ref[0, pl.ds(start, BS), :]
        s = lax.dot_general(kn, q, _NT, preferred_element_type=F32) + bias
        m_tile = jnp.max(s, axis=0, keepdims=True)
        if state is None:
            m_new = m_tile
        else:
            m_old, l_old, acc_old = state
            m_new = jnp.maximum(m_old, m_tile)
        p = jnp.exp((s - m_new) * scale)
        l_new = jnp.sum(p, axis=0, keepdims=True)
        acc_new = lax.dot_general(vn, p.astype(BF16), _TN, preferred_element_type=F32)
        if state is not None:
            alpha = jnp.exp((m_old - m_new) * scale)
            l_new = alpha * l_old + l_new
            acc_new = alpha * acc_old + acc_new
        return m_new, l_new, acc_new

    kpos = lax.broadcasted_iota(jnp.int32, (BS, BS), 0)
    qpos = lax.broadcasted_iota(jnp.int32, (BS, BS), 1)
    causal_bias = jnp.where(kpos <= qpos, 0.0, NEG_INF)
    state = tile(pl.multiple_of(qi * BS, BS), causal_bias, None)

    def body(n, state):
        n_f = n.astype(F32)
        allowed = (sel[0] == n_f) | (sel[1] == n_f) | (sel[2] == n_f)
        bias = jnp.where(allowed, 0.0, NEG_INF)
        return tile(pl.multiple_of(n * BS, BS), bias, state)

    _, l, acc = lax.fori_loop(0, qi, body, state)
    o_ref[...] = (acc * (1.0 / l)).T.astype(o_ref.dtype)


def _moba_attention(z, batch, seq_len):
    M = z.shape[1]
    BS = MOBA_BLOCK
    n_blocks = seq_len // BS
    assert seq_len % BS == 0 and n_blocks - 1 >= MOBA_TOPK
    H = ATTN_HEADS
    kern = functools.partial(_moba_kernel, n_blocks=n_blocks, scale=HEAD_DIM ** -0.5)
    return pl.pallas_call(
        kern,
        grid=(batch * H, n_blocks),
        in_specs=[
            pl.BlockSpec((1, BS, HEAD_DIM), lambda bh, qi: (SEG_AQ * H + bh % H, (bh // H) * n_blocks + qi, 0)),
            pl.BlockSpec((1, seq_len, HEAD_DIM), lambda bh, qi: (SEG_AK * H + bh % H, bh // H, 0)),
            pl.BlockSpec((1, seq_len, HEAD_DIM), lambda bh, qi: (SEG_AV * H + bh % H, bh // H, 0)),
        ],
        out_specs=pl.BlockSpec((BS, HEAD_DIM), lambda bh, qi: ((bh // H) * n_blocks + qi, bh % H)),
        out_shape=jax.ShapeDtypeStruct((M, ATTN_WIDTH), BF16),
        scratch_shapes=[pltpu.VMEM((n_blocks, HEAD_DIM), F32)],
        compiler_params=pltpu.CompilerParams(
            dimension_semantics=("parallel", "arbitrary"), vmem_limit_bytes=VMEM_LIMIT_BYTES),
        name="moba_attention",
    )(z, z, z)


def _retention_kernel(lg_ref, q_ref, k_ref, v_ref, g_ref, gret_ref, o_ref, s_scr, *, chunk, n_chunks):
    C = chunk
    head = pl.program_id(0) % RET_HEADS
    lg = lg_ref[head]

    @pl.when(pl.program_id(1) == 0)
    def _():
        s_scr[...] = jnp.zeros_like(s_scr)

    row = lax.broadcasted_iota(jnp.int32, (C, C), 0)
    col = lax.broadcasted_iota(jnp.int32, (C, C), 1)
    rel = (row - col).astype(F32)
    decay = jnp.where(rel >= 0, jnp.exp(lg * jnp.maximum(rel, 0.0)), 0.0)
    pos = lax.broadcasted_iota(jnp.int32, (C, HEAD_DIM), 0).astype(F32)
    k_decay = jnp.exp(lg * (C - 1.0 - pos))
    q_decay = jnp.exp(lg * (pos + 1.0))
    chunk_decay = jnp.exp(jnp.full((1, HEAD_DIM), lg * C, F32))
    gret = gret_ref[...]

    S = s_scr[...]
    for c in range(n_chunks):
        sl = slice(c * C, (c + 1) * C)
        q = q_ref[0, sl, :]
        k = k_ref[0, sl, :]
        v = v_ref[0, sl, :]
        s = lax.dot_general(q, k, _NT, preferred_element_type=F32) * decay
        intra = jnp.dot(s.astype(BF16), v, preferred_element_type=F32)
        cross = jnp.dot((q.astype(F32) * q_decay).astype(BF16), S.astype(BF16),
                        preferred_element_type=F32)
        r = intra + cross
        U = lax.dot_general((k.astype(F32) * k_decay).astype(BF16), v, _TN, preferred_element_type=F32)
        S = chunk_decay * S + U
        r = r * _rms_scale(r) * gret
        gt = g_ref[0, sl, :].astype(F32)
        o_ref[sl, :] = (gt * _sigmoid(gt) * r).astype(o_ref.dtype)
    s_scr[...] = S


def _retention(z, log_decay, g_ret, batch, seq_len):
    M = z.shape[1]
    H = RET_HEADS
    chunk = _pick(seq_len, 256)
    tb = _pick(seq_len, 2048)
    t_blocks = seq_len // tb

    def zspec(seg):
        return pl.BlockSpec((1, tb, HEAD_DIM), lambda bh, t: (seg * H + bh % H, (bh // H) * t_blocks + t, 0))

    kern = functools.partial(_retention_kernel, chunk=chunk, n_chunks=tb // chunk)
    return pl.pallas_call(
        kern,
        grid=(batch * H, t_blocks),
        in_specs=[
            pl.BlockSpec(memory_space=pltpu.SMEM),
            zspec(SEG_RQ), zspec(SEG_RK), zspec(SEG_RV), zspec(SEG_RG),
            pl.BlockSpec((1, HEAD_DIM), lambda bh, t: (0, bh % H)),
        ],
        out_specs=pl.BlockSpec((tb, HEAD_DIM), lambda bh, t: ((bh // H) * t_blocks + t, bh % H)),
        out_shape=jax.ShapeDtypeStruct((M, RET_WIDTH), BF16),
        scratch_shapes=[pltpu.VMEM((HEAD_DIM, HEAD_DIM), F32)],
        compiler_params=pltpu.CompilerParams(
            dimension_semantics=("parallel", "arbitrary"), vmem_limit_bytes=VMEM_LIMIT_BYTES),
        name="retention",
    )(log_decay, z, z, z, z, g_ret)


def _oproj_kernel(a_ref, r_ref, wa_ref, wr_ref, x_ref, g_ref, x1_ref, h_ref):
    mix = (jnp.dot(a_ref[...], wa_ref[...], preferred_element_type=F32)
           + jnp.dot(r_ref[...], wr_ref[...], preferred_element_type=F32))
    x1 = x_ref[...] + mix
    x1_ref[...] = x1
    h_ref[...] = (x1 * _rms_scale(x1) * g_ref[...]).astype(h_ref.dtype)


def _out_projection(a_out, r_out, w_o, x, g_ffn):
    M, D = x.shape
    tm = _pick(M, 256)
    assert ATTN_WIDTH == RET_WIDTH
    return pl.pallas_call(
        _oproj_kernel,
        grid=(M // tm,),
        in_specs=[
            pl.BlockSpec((tm, ATTN_WIDTH), lambda i: (i, 0)),
            pl.BlockSpec((tm, RET_WIDTH), lambda i: (i, 0)),
            pl.BlockSpec((ATTN_WIDTH, D), lambda i: (0, 0)),
            pl.BlockSpec((RET_WIDTH, D), lambda i: (1, 0)),
            pl.BlockSpec((tm, D), lambda i: (i, 0)),
            pl.BlockSpec((1, D), lambda i: (0, 0)),
        ],
        out_specs=[pl.BlockSpec((tm, D), lambda i: (i, 0)), pl.BlockSpec((tm, D), lambda i: (i, 0))],
        out_shape=[jax.ShapeDtypeStruct((M, D), F32), jax.ShapeDtypeStruct((M, D), BF16)],
        compiler_params=pltpu.CompilerParams(
            dimension_semantics=("parallel",), vmem_limit_bytes=VMEM_LIMIT_BYTES),
        name="out_projection",
    )(a_out, r_out, w_o, w_o, x, g_ffn)


def _ffn_kernel(h_ref, x1_ref, wg_ref, wu_ref, wd_ref, g_ref, x2_ref, h3_ref):
    f = pl.program_id(1)
    h = h_ref[...]
    gate = jnp.dot(h, wg_ref[...], preferred_element_type=F32)
    up = jnp.dot(h, wu_ref[...], preferred_element_type=F32)
    act = (gate * _sigmoid(gate) * up).astype(BF16)
    part = jnp.dot(act, wd_ref[...], preferred_element_type=F32)

    @pl.when(f == 0)
    def _():
        x2_ref[...] = x1_ref[...] + part

    @pl.when(f > 0)
    def _():
        x2_ref[...] += part

    @pl.when(f == pl.num_programs(1) - 1)
    def _():
        x2 = x2_ref[...]
        h3_ref[...] = (x2 * _rms_scale(x2) * g_ref[...]).astype(h3_ref.dtype)


def _ffn(h2, x1, w_gate, w_up, w_down, g_ple):
    M, D = x1.shape
    F = w_gate.shape[1]
    tm = _pick(M, 512)
    tf = _pick(F, 512)
    return pl.pallas_call(
        _ffn_kernel,
        grid=(M // tm, F // tf),
        in_specs=[
            pl.BlockSpec((tm, D), lambda i, f: (i, 0)),
            pl.BlockSpec((tm, D), lambda i, f: (i, 0)),
            pl.BlockSpec((D, tf), lambda i, f: (0, f)),
            pl.BlockSpec((D, tf), lambda i, f: (0, f)),
            pl.BlockSpec((tf, D), lambda i, f: (f, 0)),
            pl.BlockSpec((1, D), lambda i, f: (0, 0)),
        ],
        out_specs=[pl.BlockSpec((tm, D), lambda i, f: (i, 0)), pl.BlockSpec((tm, D), lambda i, f: (i, 0))],
        out_shape=[jax.ShapeDtypeStruct((M, D), F32), jax.ShapeDtypeStruct((M, D), BF16)],
        compiler_params=pltpu.CompilerParams(
            dimension_semantics=("parallel", "arbitrary"), vmem_limit_bytes=VMEM_LIMIT_BYTES),
        name="swiglu_ffn",
    )(h2, x1, w_gate, w_up, w_down, g_ple)


def _ple_kernel(h_ref, p_ref, x2_ref, wg_ref, b_ref, wp_ref, o_ref):
    gate = _sigmoid(jnp.dot(h_ref[...], wg_ref[...], preferred_element_type=F32) + b_ref[...])
    proj = jnp.dot(p_ref[...].astype(BF16), wp_ref[...], preferred_element_type=F32)
    o_ref[...] = x2_ref[...] + proj * gate


def _ple(h3, p, x2, w_gate, b_gate, w_proj):
    M, D = x2.shape
    P = p.shape[1]
    tm = _pick(M, 256)
    return pl.pallas_call(
        _ple_kernel,
        grid=(M // tm,),
        in_specs=[
            pl.BlockSpec((tm, D), lambda i: (i, 0)),
            pl.BlockSpec((tm, P), lambda i: (i, 0)),
            pl.BlockSpec((tm, D), lambda i: (i, 0)),
            pl.BlockSpec((D, D), lambda i: (0, 0)),
            pl.BlockSpec((1, D), lambda i: (0, 0)),
            pl.BlockSpec((P, D), lambda i: (0, 0)),
        ],
        out_specs=pl.BlockSpec((tm, D), lambda i: (i, 0)),
        out_shape=jax.ShapeDtypeStruct((M, D), F32),
        compiler_params=pltpu.CompilerParams(
            dimension_semantics=("parallel",), vmem_limit_bytes=VMEM_LIMIT_BYTES),
        name="gated_ple",
    )(h3, p, x2, w_gate, b_gate, w_proj)


def _rope_tables(seq_len):
    pos = jnp.arange(seq_len, dtype=F32)
    inv_attn = jnp.power(ROPE_THETA, -jnp.arange(0, HEAD_DIM, 2, dtype=F32) / HEAD_DIM)
    inv_ret = jnp.power(ROPE_THETA, -jnp.linspace(0.0, 1.0, HEAD_DIM // 2, dtype=F32))

    def table(inv_freq):
        ang = pos[:, None] * inv_freq[None, :]
        c, s = jnp.cos(ang), jnp.sin(ang)
        return jnp.concatenate([c, c], axis=-1), jnp.concatenate([-s, s], axis=-1)

    ca, sa = table(inv_attn)
    cr, sr = table(inv_ret)
    cos_tab = jnp.stack([ca, cr, jnp.ones_like(ca)])
    sin_tab = jnp.stack([sa, sr, jnp.zeros_like(sa)])
    return cos_tab, sin_tab


def kernel(x, p, g_mix, w_in, q_norm, k_norm, g_ret, w_o, g_ffn, w_gate, w_up, w_down, g_ple,
           w_ple_gate, b_ple_gate, w_ple_proj):
    B, T, D = x.shape
    depth = w_in.shape[0]
    M = B * T
    assert w_in.shape[2] == 3 * ATTN_WIDTH + 4 * RET_WIDTH and ATTN_WIDTH == RET_WIDTH

    cos_tab, sin_tab = _rope_tables(T)
    log_decay = jnp.log(1.0 - jnp.power(2.0, -5.0 - jnp.arange(RET_HEADS, dtype=F32)))
    ones = jnp.ones((HEAD_DIM,), F32)

    xf = x.reshape(M, D)
    for i in range(depth):
        gains = jnp.stack([q_norm[i], k_norm[i], ones, ones, ones * (HEAD_DIM ** -0.5), ones, ones])
        z = _in_projection(xf, g_mix[i][None], w_in[i].astype(BF16), gains[:, None, :],
                           cos_tab, sin_tab, T)
        a_out = _moba_attention(z, B, T)
        r_out = _retention(z, log_decay, g_ret[i][None], B, T)
        x1, h2 = _out_projection(a_out, r_out, w_o[i].astype(BF16), xf, g_ffn[i][None])
        x2, h3 = _ffn(h2, x1, w_gate[i].astype(BF16), w_up[i].astype(BF16), w_down[i].astype(BF16),
                      g_ple[i][None])
        xf = _ple(h3, p[i].reshape(M, -1), x2, w_ple_gate[i].astype(BF16), b_ple_gate[i][None],
                  w_ple_proj[i].astype(BF16))
    return xf.reshape(B, T, D)
```

```python
import functools

import jax
import jax.numpy as jnp
from jax import lax
from jax.experimental import pallas as pl
from jax.experimental.pallas import tpu as pltpu

HEAD_DIM = 128
ATTN_HEADS = 8
RET_HEADS = 8
ATTN_WIDTH = ATTN_HEADS * HEAD_DIM
RET_WIDTH = RET_HEADS * HEAD_DIM
MOBA_BLOCK = 256
MOBA_TOPK = 3
ROPE_THETA = 10000.0
RMS_EPS = 1e-6
NEG_INF = -1e30
LOG2_E = 1.4426950408889634

N_SEGMENTS = 7
SEG_AQ, SEG_AK, SEG_AV, SEG_RQ, SEG_RK, SEG_RV, SEG_RG = range(N_SEGMENTS)
ROPE_ATTN, ROPE_RET = 0, 1

LANES = 128
VMEM_LIMIT_BYTES = 56 * 1024 * 1024

F32 = jnp.float32
BF16 = jnp.bfloat16

_NT = (((1,), (1,)), ((), ()))
_TN = (((0,), (0,)), ((), ()))


def _sigmoid(x):
    return 1.0 / (1.0 + jnp.exp(-x))


def _rms_scale(x):
    return lax.rsqrt(jnp.mean(x * x, axis=-1, keepdims=True) + RMS_EPS)


def _pick(dim, pref):
    t = min(dim, pref)
    while dim % t:
        t //= 2
    return t


def _inproj_kernel(x_ref, gmix_ref, w_ref, gain_ref, cos_ref, sin_ref, z_ref, h_scr, *,
                   blocks_per_seg, slab):
    j = pl.program_id(1)
    seg = j // blocks_per_seg

    @pl.when(j == 0)
    def _():
        x = x_ref[...]
        h_scr[...] = (x * _rms_scale(x) * gmix_ref[...]).astype(h_scr.dtype)

    def step(norm, rope):
        heads_per_slab = slab // HEAD_DIM
        n_slabs = w_ref.shape[1] // slab

        def slab_dot(s):
            return jnp.dot(h_scr[...], w_ref[:, s * slab:(s + 1) * slab], preferred_element_type=F32)

        acc_next = slab_dot(0)
        for s in range(n_slabs):
            acc = acc_next
            if s + 1 < n_slabs:
                acc_next = slab_dot(s + 1)
            for hh in range(heads_per_slab):
                y = acc[:, hh * HEAD_DIM:(hh + 1) * HEAD_DIM]
                out = y
                if rope:
                    y = y * gain_ref[0]
                    out = y * cos_ref[0] + pltpu.roll(y, HEAD_DIM // 2, 1) * sin_ref[0]
                if norm:
                    out = out * _rms_scale(acc[:, hh * HEAD_DIM:(hh + 1) * HEAD_DIM])
                z_ref[s * heads_per_slab + hh] = out.astype(z_ref.dtype)

    is_attn_qk = seg <= SEG_AK
    is_ret_qk = (seg == SEG_RQ) | (seg == SEG_RK)
    pl.when(is_attn_qk)(lambda: step(norm=True, rope=True))
    pl.when(is_ret_qk)(lambda: step(norm=False, rope=True))
    pl.when(jnp.logical_not(is_attn_qk | is_ret_qk))(lambda: step(norm=False, rope=False))


def _in_projection(x, g_mix, w_in, gains, cos_tab, sin_tab, seq_len):
    M, D = x.shape
    n_cols = w_in.shape[1]
    tm = _pick(seq_len, 1024)
    tn = _pick(ATTN_WIDTH, 1024)
    blocks_per_seg = ATTN_WIDTH // tn
    t_blocks = seq_len // tm

    def seg_of(j):
        return j // blocks_per_seg

    def rope_of(j):
        return jnp.where(seg_of(j) <= SEG_AV, ROPE_ATTN, ROPE_RET)

    kern = functools.partial(_inproj_kernel, blocks_per_seg=blocks_per_seg, slab=_pick(tn, 256))
    return pl.pallas_call(
        kern,
        grid=(M // tm, n_cols // tn),
        in_specs=[
            pl.BlockSpec((tm, D), lambda i, j: (i, 0)),
            pl.BlockSpec((1, D), lambda i, j: (0, 0)),
            pl.BlockSpec((D, tn), lambda i, j: (0, j)),
            pl.BlockSpec((1, 1, HEAD_DIM), lambda i, j: (seg_of(j), 0, 0)),
            pl.BlockSpec((1, tm, HEAD_DIM), lambda i, j: (rope_of(j), i % t_blocks, 0)),
            pl.BlockSpec((1, tm, HEAD_DIM), lambda i, j: (rope_of(j), i % t_blocks, 0)),
        ],
        out_specs=pl.BlockSpec((tn // HEAD_DIM, tm, HEAD_DIM), lambda i, j: (j, i, 0)),
        out_shape=jax.ShapeDtypeStruct((n_cols // HEAD_DIM, M, HEAD_DIM), BF16),
        scratch_shapes=[pltpu.VMEM((tm, D), BF16)],
        compiler_params=pltpu.CompilerParams(
            dimension_semantics=("parallel", "arbitrary"), vmem_limit_bytes=VMEM_LIMIT_BYTES),
        name="in_projection",
    )(x, g_mix, w_in, gains, cos_tab, sin_tab)


def _moba_kernel(q_ref, k_ref, v_ref, o_ref, kmean_scr, sa_scr, sb_scr, *, n_blocks, group, heads):
    BS = MOBA_BLOCK
    G = group
    qi = pl.program_id(1)

    @pl.when(qi == 0)
    def _():
        for hh in range(heads):
            for n in range(n_blocks):
                kb = k_ref[hh, n * BS:(n + 1) * BS, :].astype(F32)
                kmean_scr[hh, n:n + 1, :] = jnp.mean(kb, axis=0, keepdims=True)

    kpos = lax.broadcasted_iota(jnp.int32, (BS, BS), 0)
    qpos = lax.broadcasted_iota(jnp.int32, (BS, BS), 1)
    causal_bias = jnp.where(kpos <= qpos, 0.0, NEG_INF)

    def select_blocks(q, hh):
        gate = lax.dot_general(kmean_scr[hh].astype(BF16), q, _NT, preferred_element_type=F32)
        blk = lax.broadcasted_iota(jnp.int32, gate.shape, 0)
        blk_f = blk.astype(F32)
        gate = jnp.where(blk < qi, gate, NEG_INF)
        sel = []
        for r in range(MOBA_TOPK):
            mx = jnp.max(gate, axis=0, keepdims=True)
            pick = jnp.min(jnp.where(gate == mx, blk_f, float(n_blocks)), axis=0, keepdims=True)
            sel.append(jnp.where(r < qi, pick, -1.0))
            gate = jnp.where(blk_f == pick, -jnp.inf, gate)
        return sel

    def row_bias(sel, n):
        n_f = lax.convert_element_type(n, F32)
        allowed = (sel[0] == n_f) | (sel[1] == n_f) | (sel[2] == n_f)
        return jnp.where(allowed, 0.0, NEG_INF)

    qs = [q_ref[hh] for hh in range(heads)]
    all_heads = range(heads)

    def score_dots(start, size):
        return [lax.dot_general(k_ref[hh, pl.ds(start, size), :], qs[hh], _NT, preferred_element_type=F32)
                for hh in all_heads]

    def group_start(g):
        return pl.multiple_of(g * (G * BS), G * BS)

    def score_finish(g, ss, buf, m_run):
        m_new, alpha = [], []
        for hh in all_heads:
            buf[hh] = ss[hh]
            m = m_run[hh]
            for t in range(G):
                tile_max = jnp.max(ss[hh][t * BS:(t + 1) * BS], axis=0, keepdims=True)
                m = jnp.maximum(m, tile_max + row_bias(sels[hh], g * G + t))
            m_new.append(m)
            alpha.append(jnp.exp2(m_run[hh] - m))
        return m_new, alpha

    def score_stage(g, buf, m_run):
        return score_finish(g, score_dots(group_start(g), G * BS), buf, m_run)

    def value_stage(g, buf, m_g, alpha, l, acc):
        start = group_start(g)
        l_new, pbs = [], []
        for hh in all_heads:
            p = [jnp.exp2(buf[hh, t * BS:(t + 1) * BS, :] + (row_bias(sels[hh], g * G + t) - m_g[hh]))
                 for t in range(G)]
            l_new.append(functools.reduce(
                jnp.add, [jnp.sum(pt, axis=0, keepdims=True) for pt in p], alpha[hh] * l[hh]))
            pbs.append(jnp.concatenate([pt.astype(BF16) for pt in p], axis=0))
        acc_new = [alpha[hh] * acc[hh]
                   + lax.dot_general(v_ref[hh, pl.ds(start, G * BS), :], pbs[hh], _TN, preferred_element_type=F32)
                   for hh in all_heads]
        return l_new, acc_new

    n_pairs = jnp.maximum((qi + 2 * G - 1) // (2 * G), 1)

    own_start = pl.multiple_of(qi * BS, BS)
    ss_own = score_dots(own_start, BS)
    ss_0 = score_dots(group_start(0), G * BS)
    sels = [select_blocks(qs[hh], hh) for hh in all_heads]
    ss_own = [s + causal_bias for s in ss_own]
    ms = [jnp.max(s, axis=0, keepdims=True) for s in ss_own]
    ps = [jnp.exp2(s - m) for s, m in zip(ss_own, ms)]
    l = [jnp.sum(p, axis=0, keepdims=True) for p in ps]
    m_a, alpha_a = score_finish(0, ss_0, sa_scr, ms)
    acc = [lax.dot_general(v_ref[hh, pl.ds(own_start, BS), :], ps[hh].astype(BF16), _TN,
                           preferred_element_type=F32) for hh in all_heads]

    def pair(k, carry):
        m_a, alpha_a, l, acc = carry
        m_b, alpha_b = score_stage(2 * k + 1, sb_scr, m_a)
        l, acc = value_stage(2 * k, sa_scr, m_a, alpha_a, l, acc)
        m_n, alpha_n = score_stage(2 * k + 2, sa_scr, m_b)
        l, acc = value_stage(2 * k + 1, sb_scr, m_b, alpha_b, l, acc)
        return m_n, alpha_n, l, acc

    m_a, alpha_a, l, acc = lax.fori_loop(0, n_pairs - 1, pair, (m_a, alpha_a, l, acc))
    last = 2 * (n_pairs - 1)
    m_b, alpha_b = score_stage(last + 1, sb_scr, m_a)
    l, acc = value_stage(last, sa_scr, m_a, alpha_a, l, acc)
    l, acc = value_stage(last + 1, sb_scr, m_b, alpha_b, l, acc)
    for hh in all_heads:
        o_ref[:, hh * HEAD_DIM:(hh + 1) * HEAD_DIM] = (acc[hh] * (1.0 / l[hh])).T.astype(o_ref.dtype)


def _moba_attention(z, batch, seq_len):
    M = z.shape[1]
    BS = MOBA_BLOCK
    n_blocks = seq_len // BS
    H = ATTN_HEADS
    heads = 4
    group = 2
    assert seq_len % BS == 0 and n_blocks - 1 >= MOBA_TOPK and n_blocks % (2 * group) == 0
    hb = H // heads
    kern = functools.partial(_moba_kernel, n_blocks=n_blocks, group=group, heads=heads)
    return pl.pallas_call(
        kern,
        grid=(batch * hb, n_blocks),
        in_specs=[
            pl.BlockSpec((heads, BS, HEAD_DIM), lambda bh, qi: (SEG_AQ * hb + bh % hb, (bh // hb) * n_blocks + qi, 0)),
            pl.BlockSpec((heads, seq_len, HEAD_DIM), lambda bh, qi: (SEG_AK * hb + bh % hb, bh // hb, 0)),
            pl.BlockSpec((heads, seq_len, HEAD_DIM), lambda bh, qi: (SEG_AV * hb + bh % hb, bh // hb, 0)),
        ],
        out_specs=pl.BlockSpec((BS, heads * HEAD_DIM), lambda bh, qi: ((bh // hb) * n_blocks + qi, bh % hb)),
        out_shape=jax.ShapeDtypeStruct((M, ATTN_WIDTH), BF16),
        scratch_shapes=[pltpu.VMEM((heads, n_blocks, HEAD_DIM), F32),
                        pltpu.VMEM((heads, group * BS, BS), F32),
                        pltpu.VMEM((heads, group * BS, BS), F32)],
        compiler_params=pltpu.CompilerParams(
            dimension_semantics=("parallel", "arbitrary"), vmem_limit_bytes=VMEM_LIMIT_BYTES),
        name="moba_attention",
    )(z, z, z)


def _retention_kernel(lg_ref, q_ref, k_ref, v_ref, g_ref, gret_ref, o_ref, s_scr, *, chunk, n_chunks):
    C = chunk
    head = pl.program_id(0) % RET_HEADS
    lg = lg_ref[head]

    @pl.when(pl.program_id(1) == 0)
    def _():
        s_scr[...] = jnp.zeros_like(s_scr)

    row = lax.broadcasted_iota(jnp.int32, (C, C), 0)
    col = lax.broadcasted_iota(jnp.int32, (C, C), 1)
    rel = (row - col).astype(F32)
    decay = jnp.where(rel >= 0, jnp.exp(lg * jnp.maximum(rel, 0.0)), 0.0)
    pos = lax.broadcasted_iota(jnp.int32, (C, HEAD_DIM), 0).astype(F32)
    k_decay = jnp.exp(lg * (C - 1.0 - pos))
    q_decay = jnp.exp(lg * (pos + 1.0))
    chunk_decay = jnp.exp(jnp.full((1, HEAD_DIM), lg * C, F32))
    gret = gret_ref[...]

    S = s_scr[...]
    for c in range(n_chunks):
        sl = slice(c * C, (c + 1) * C)
        q = q_ref[0, sl, :]
        k = k_ref[0, sl, :]
        v = v_ref[0, sl, :]
        s = lax.dot_general(q, k, _NT, preferred_element_type=F32) * decay
        intra = jnp.dot(s.astype(BF16), v, preferred_element_type=F32)
        cross = jnp.dot((q.astype(F32) * q_decay).astype(BF16), S.astype(BF16),
                        preferred_element_type=F32)
        r = intra + cross
        U = lax.dot_general((k.astype(F32) * k_decay).astype(BF16), v, _TN, preferred_element_type=F32)
        S = chunk_decay * S + U
        r = r * _rms_scale(r) * gret
        gt = g_ref[0, sl, :].astype(F32)
        o_ref[sl, :] = (gt * _sigmoid(gt) * r).astype(o_ref.dtype)
    s_scr[...] = S


def _retention(z, log_decay, g_ret, batch, seq_len):
    M = z.shape[1]
    H = RET_HEADS
    chunk = _pick(seq_len, 256)
    tb = _pick(seq_len, 2048)
    t_blocks = seq_len // tb

    def zspec(seg):
        return pl.BlockSpec((1, tb, HEAD_DIM), lambda bh, t: (seg * H + bh % H, (bh // H) * t_blocks + t, 0))

    kern = functools.partial(_retention_kernel, chunk=chunk, n_chunks=tb // chunk)
    return pl.pallas_call(
        kern,
        grid=(batch * H, t_blocks),
        in_specs=[
            pl.BlockSpec(memory_space=pltpu.SMEM),
            zspec(SEG_RQ), zspec(SEG_RK), zspec(SEG_RV), zspec(SEG_RG),
            pl.BlockSpec((1, HEAD_DIM), lambda bh, t: (0, bh % H)),
        ],
        out_specs=pl.BlockSpec((tb, HEAD_DIM), lambda bh, t: ((bh // H) * t_blocks + t, bh % H)),
        out_shape=jax.ShapeDtypeStruct((M, RET_WIDTH), BF16),
        scratch_shapes=[pltpu.VMEM((HEAD_DIM, HEAD_DIM), F32)],
        compiler_params=pltpu.CompilerParams(
            dimension_semantics=("parallel", "arbitrary"), vmem_limit_bytes=VMEM_LIMIT_BYTES),
        name="retention",
    )(log_decay, z, z, z, z, g_ret)


def _oproj_kernel(a_ref, r_ref, wa_ref, wr_ref, x_ref, g_ref, x1_ref, h_ref):
    mix = (jnp.dot(a_ref[...], wa_ref[...], preferred_element_type=F32)
           + jnp.dot(r_ref[...], wr_ref[...], preferred_element_type=F32))
    x1 = x_ref[...] + mix
    x1_ref[...] = x1
    h_ref[...] = (x1 * _rms_scale(x1) * g_ref[...]).astype(h_ref.dtype)


def _out_projection(a_out, r_out, w_o, x, g_ffn):
    M, D = x.shape
    tm = _pick(M, 256)
    assert ATTN_WIDTH == RET_WIDTH
    return pl.pallas_call(
        _oproj_kernel,
        grid=(M // tm,),
        in_specs=[
            pl.BlockSpec((tm, ATTN_WIDTH), lambda i: (i, 0)),
            pl.BlockSpec((tm, RET_WIDTH), lambda i: (i, 0)),
            pl.BlockSpec((ATTN_WIDTH, D), lambda i: (0, 0)),
            pl.BlockSpec((RET_WIDTH, D), lambda i: (1, 0)),
            pl.BlockSpec((tm, D), lambda i: (i, 0)),
            pl.BlockSpec((1, D), lambda i: (0, 0)),
        ],
        out_specs=[pl.BlockSpec((tm, D), lambda i: (i, 0)), pl.BlockSpec((tm, D), lambda i: (i, 0))],
        out_shape=[jax.ShapeDtypeStruct((M, D), F32), jax.ShapeDtypeStruct((M, D), BF16)],
        compiler_params=pltpu.CompilerParams(
            dimension_semantics=("parallel",), vmem_limit_bytes=VMEM_LIMIT_BYTES),
        name="out_projection",
    )(a_out, r_out, w_o, w_o, x, g_ffn)


def _ffn_kernel(h_ref, x1_ref, wg_ref, wu_ref, wd_ref, g_ref, x2_ref, h3_ref):
    f = pl.program_id(1)
    h = h_ref[...]
    gate = jnp.dot(h, wg_ref[...], preferred_element_type=F32)
    up = jnp.dot(h, wu_ref[...], preferred_element_type=F32)
    act = (gate * _sigmoid(gate) * up).astype(BF16)
    part = jnp.dot(act, wd_ref[...], preferred_element_type=F32)

    @pl.when(f == 0)
    def _():
        x2_ref[...] = x1_ref[...] + part

    @pl.when(f > 0)
    def _():
        x2_ref[...] += part

    @pl.when(f == pl.num_programs(1) - 1)
    def _():
        x2 = x2_ref[...]
        h3_ref[...] = (x2 * _rms_scale(x2) * g_ref[...]).astype(h3_ref.dtype)


def _ffn(h2, x1, w_gate, w_up, w_down, g_ple):
    M, D = x1.shape
    F = w_gate.shape[1]
    tm = _pick(M, 512)
    tf = _pick(F, 512)
    return pl.pallas_call(
        _ffn_kernel,
        grid=(M // tm, F // tf),
        in_specs=[
            pl.BlockSpec((tm, D), lambda i, f: (i, 0)),
            pl.BlockSpec((tm, D), lambda i, f: (i, 0)),
            pl.BlockSpec((D, tf), lambda i, f: (0, f)),
            pl.BlockSpec((D, tf), lambda i, f: (0, f)),
            pl.BlockSpec((tf, D), lambda i, f: (f, 0)),
            pl.BlockSpec((1, D), lambda i, f: (0, 0)),
        ],
        out_specs=[pl.BlockSpec((tm, D), lambda i, f: (i, 0)), pl.BlockSpec((tm, D), lambda i, f: (i, 0))],
        out_shape=[jax.ShapeDtypeStruct((M, D), F32), jax.ShapeDtypeStruct((M, D), BF16)],
        compiler_params=pltpu.CompilerParams(
            dimension_semantics=("parallel", "arbitrary"), vmem_limit_bytes=VMEM_LIMIT_BYTES),
        name="swiglu_ffn",
    )(h2, x1, w_gate, w_up, w_down, g_ple)


def _ple_kernel(h_ref, p_ref, x2_ref, wg_ref, b_ref, wp_ref, o_ref):
    gate = _sigmoid(jnp.dot(h_ref[...], wg_ref[...], preferred_element_type=F32) + b_ref[...])
    proj = jnp.dot(p_ref[...].astype(BF16), wp_ref[...], preferred_element_type=F32)
    o_ref[...] = x2_ref[...] + proj * gate


def _ple(h3, p, x2, w_gate, b_gate, w_proj):
    M, D = x2.shape
    P = p.shape[1]
    tm = _pick(M, 256)
    return pl.pallas_call(
        _ple_kernel,
        grid=(M // tm,),
        in_specs=[
            pl.BlockSpec((tm, D), lambda i: (i, 0)),
            pl.BlockSpec((tm, P), lambda i: (i, 0)),
            pl.BlockSpec((tm, D), lambda i: (i, 0)),
            pl.BlockSpec((D, D), lambda i: (0, 0)),
            pl.BlockSpec((1, D), lambda i: (0, 0)),
            pl.BlockSpec((P, D), lambda i: (0, 0)),
        ],
        out_specs=pl.BlockSpec((tm, D), lambda i: (i, 0)),
        out_shape=jax.ShapeDtypeStruct((M, D), F32),
        compiler_params=pltpu.CompilerParams(
            dimension_semantics=("parallel",), vmem_limit_bytes=VMEM_LIMIT_BYTES),
        name="gated_ple",
    )(h3, p, x2, w_gate, b_gate, w_proj)


def _rope_tables(seq_len):
    pos = jnp.arange(seq_len, dtype=F32)
    inv_attn = jnp.power(ROPE_THETA, -jnp.arange(0, HEAD_DIM, 2, dtype=F32) / HEAD_DIM)
    inv_ret = jnp.power(ROPE_THETA, -jnp.linspace(0.0, 1.0, HEAD_DIM // 2, dtype=F32))

    def table(inv_freq):
        ang = pos[:, None] * inv_freq[None, :]
        c, s = jnp.cos(ang), jnp.sin(ang)
        return jnp.concatenate([c, c], axis=-1), jnp.concatenate([-s, s], axis=-1)

    ca, sa = table(inv_attn)
    cr, sr = table(inv_ret)
    return jnp.stack([ca, cr]), jnp.stack([sa, sr])


def kernel(x, p, g_mix, w_in, q_norm, k_norm, g_ret, w_o, g_ffn, w_gate, w_up, w_down, g_ple,
           w_ple_gate, b_ple_gate, w_ple_proj):
    B, T, D = x.shape
    depth = w_in.shape[0]
    M = B * T
    assert w_in.shape[2] == 3 * ATTN_WIDTH + 4 * RET_WIDTH and ATTN_WIDTH == RET_WIDTH

    cos_tab, sin_tab = _rope_tables(T)
    log_decay = jnp.log(1.0 - jnp.power(2.0, -5.0 - jnp.arange(RET_HEADS, dtype=F32)))
    ones = jnp.ones((HEAD_DIM,), F32)

    xf = x.reshape(M, D)
    for i in range(depth):
        gains = jnp.stack([q_norm[i] * (LOG2_E * HEAD_DIM ** -0.5), k_norm[i], ones, ones,
                           ones * (HEAD_DIM ** -0.5), ones, ones])
        z = _in_projection(xf, g_mix[i][None], w_in[i].astype(BF16), gains[:, None, :],
                           cos_tab, sin_tab, T)
        a_out = _moba_attention(z, B, T)
        r_out = _retention(z, log_decay, g_ret[i][None], B, T)
        x1, h2 = _out_projection(a_out, r_out, w_o[i].astype(BF16), xf, g_ffn[i][None])
        x2, h3 = _ffn(h2, x1, w_gate[i].astype(BF16), w_up[i].astype(BF16), w_down[i].astype(BF16),
                      g_ple[i][None])
        xf = _ple(h3, p[i].reshape(M, -1), x2, w_ple_gate[i].astype(BF16), b_ple_gate[i][None],
                  w_ple_proj[i].astype(BF16))
    return xf.reshape(B, T, D)
```

```python
import functools

import jax
import jax.numpy as jnp
from jax import lax
from jax.experimental import pallas as pl
from jax.experimental.pallas import tpu as pltpu

HEAD_DIM = 128
ATTN_HEADS = 8
RET_HEADS = 8
ATTN_WIDTH = ATTN_HEADS * HEAD_DIM
RET_WIDTH = RET_HEADS * HEAD_DIM
MOBA_BLOCK = 256
MOBA_TOPK = 3
ROPE_THETA = 10000.0
RMS_EPS = 1e-6
NEG_INF = -1e30
LOG2_E = 1.4426950408889634

N_SEGMENTS = 7
SEG_AQ, SEG_AK, SEG_AV, SEG_RQ, SEG_RK, SEG_RV, SEG_RG = range(N_SEGMENTS)
ROPE_ATTN, ROPE_RET = 0, 1

LANES = 128
VMEM_LIMIT_BYTES = 56 * 1024 * 1024

F32 = jnp.float32
BF16 = jnp.bfloat16

_NT = (((1,), (1,)), ((), ()))
_TN = (((0,), (0,)), ((), ()))


def _sigmoid(x):
    return 1.0 / (1.0 + jnp.exp(-x))


def _rms_scale(x):
    return lax.rsqrt(jnp.mean(x * x, axis=-1, keepdims=True) + RMS_EPS)


def _pick(dim, pref):
    t = min(dim, pref)
    while dim % t:
        t //= 2
    return t


def _inproj_kernel(x_ref, gmix_ref, w_ref, gain_ref, cos_ref, sin_ref, z_ref, h_scr, *,
                   blocks_per_seg, slab):
    j = pl.program_id(1)
    seg = j // blocks_per_seg

    @pl.when(j == 0)
    def _():
        x = x_ref[...]
        h_scr[...] = (x * _rms_scale(x) * gmix_ref[...]).astype(h_scr.dtype)

    def step(norm, rope):
        heads_per_slab = slab // HEAD_DIM
        n_slabs = w_ref.shape[1] // slab
        ones_bf = jnp.ones((HEAD_DIM, HEAD_DIM), BF16)

        def slab_dot(s):
            return jnp.dot(h_scr[...], w_ref[:, s * slab:(s + 1) * slab], preferred_element_type=F32)

        acc_next = slab_dot(0)
        for s in range(n_slabs):
            acc = acc_next
            if s + 1 < n_slabs:
                acc_next = slab_dot(s + 1)
            for hh in range(heads_per_slab):
                y = acc[:, hh * HEAD_DIM:(hh + 1) * HEAD_DIM]
                out = y
                if rope:
                    y = y * gain_ref[0]
                    out = y * cos_ref[0] + pltpu.roll(y, HEAD_DIM // 2, 1) * sin_ref[0]
                if norm:
                    y0 = acc[:, hh * HEAD_DIM:(hh + 1) * HEAD_DIM]
                    sq = y0 * y0
                    hi = sq.astype(BF16)
                    lo = (sq - hi.astype(F32)).astype(BF16)
                    ssum = (jnp.dot(hi, ones_bf, preferred_element_type=F32)
                            + jnp.dot(lo, ones_bf, preferred_element_type=F32))
                    out = out * lax.rsqrt(ssum * (1.0 / HEAD_DIM) + RMS_EPS)
                z_ref[s * heads_per_slab + hh] = out.astype(z_ref.dtype)

    is_attn_qk = seg <= SEG_AK
    is_ret_qk = (seg == SEG_RQ) | (seg == SEG_RK)
    pl.when(is_attn_qk)(lambda: step(norm=True, rope=True))
    pl.when(is_ret_qk)(lambda: step(norm=False, rope=True))
    pl.when(jnp.logical_not(is_attn_qk | is_ret_qk))(lambda: step(norm=False, rope=False))


def _in_projection(x, g_mix, w_in, gains, cos_tab, sin_tab, seq_len):
    M, D = x.shape
    n_cols = w_in.shape[1]
    tm = _pick(seq_len, 1024)
    tn = _pick(ATTN_WIDTH, 1024)
    blocks_per_seg = ATTN_WIDTH // tn
    t_blocks = seq_len // tm

    def seg_of(j):
        return j // blocks_per_seg

    def rope_of(j):
        return jnp.where(seg_of(j) <= SEG_AV, ROPE_ATTN, ROPE_RET)

    kern = functools.partial(_inproj_kernel, blocks_per_seg=blocks_per_seg, slab=_pick(tn, 256))
    return pl.pallas_call(
        kern,
        grid=(M // tm, n_cols // tn),
        in_specs=[
            pl.BlockSpec((tm, D), lambda i, j: (i, 0)),
            pl.BlockSpec((1, D), lambda i, j: (0, 0)),
            pl.BlockSpec((D, tn), lambda i, j: (0, j)),
            pl.BlockSpec((1, 1, HEAD_DIM), lambda i, j: (seg_of(j), 0, 0)),
            pl.BlockSpec((1, tm, HEAD_DIM), lambda i, j: (rope_of(j), i % t_blocks, 0)),
            pl.BlockSpec((1, tm, HEAD_DIM), lambda i, j: (rope_of(j), i % t_blocks, 0)),
        ],
        out_specs=pl.BlockSpec((tn // HEAD_DIM, tm, HEAD_DIM), lambda i, j: (j, i, 0)),
        out_shape=jax.ShapeDtypeStruct((n_cols // HEAD_DIM, M, HEAD_DIM), BF16),
        scratch_shapes=[pltpu.VMEM((tm, D), BF16)],
        compiler_params=pltpu.CompilerParams(
            dimension_semantics=("parallel", "arbitrary"), vmem_limit_bytes=VMEM_LIMIT_BYTES),
        name="in_projection",
    )(x, g_mix, w_in, gains, cos_tab, sin_tab)


def _moba_kernel(q_ref, k_ref, v_ref, o_ref, kmean_scr, vt_scr, sa_scr, sb_scr, *, n_blocks, group, heads):
    BS = MOBA_BLOCK
    G = group
    qi = pl.program_id(1)

    @pl.when(qi == 0)
    def _():
        for hh in range(heads):
            for n in range(n_blocks):
                kb = k_ref[hh, n * BS:(n + 1) * BS, :].astype(F32)
                kmean_scr[hh, n:n + 1, :] = jnp.mean(kb, axis=0, keepdims=True)

            def fill(c, carry, hh=hh):
                start = pl.multiple_of(c * (G * BS), G * BS)
                vt_scr[hh, :, pl.ds(start, G * BS)] = (
                    v_ref[hh, pl.ds(start, G * BS), :].astype(F32).T.astype(vt_scr.dtype))
                return carry

            lax.fori_loop(0, n_blocks // G, fill, 0)

    kpos = lax.broadcasted_iota(jnp.int32, (BS, BS), 0)
    qpos = lax.broadcasted_iota(jnp.int32, (BS, BS), 1)
    causal_bias = jnp.where(kpos <= qpos, 0.0, NEG_INF)

    def select_blocks(q, hh):
        gate = lax.dot_general(kmean_scr[hh].astype(BF16), q, _NT, preferred_element_type=F32)
        blk = lax.broadcasted_iota(jnp.int32, gate.shape, 0)
        blk_f = blk.astype(F32)
        gate = jnp.where(blk < qi, gate, NEG_INF)
        sel = []
        for r in range(MOBA_TOPK):
            mx = jnp.max(gate, axis=0, keepdims=True)
            pick = jnp.min(jnp.where(gate == mx, blk_f, float(n_blocks)), axis=0, keepdims=True)
            sel.append(jnp.where(r < qi, pick, -1.0))
            gate = jnp.where(blk_f == pick, -jnp.inf, gate)
        return sel

    def row_bias(sel, n):
        n_f = lax.convert_element_type(n, F32)
        allowed = (sel[0] == n_f) | (sel[1] == n_f) | (sel[2] == n_f)
        return jnp.where(allowed, 0.0, NEG_INF)

    qs = [q_ref[hh] for hh in range(heads)]
    qts = [q.astype(F32).T.astype(BF16) for q in qs]
    all_heads = range(heads)

    def score_dots(start, size):
        return [jnp.dot(k_ref[hh, pl.ds(start, size), :], qts[hh], preferred_element_type=F32)
                for hh in all_heads]

    def value_dot(hh, start, size, pb):
        return jnp.dot(vt_scr[hh, :, pl.ds(start, size)], pb, preferred_element_type=F32)

    def group_start(g):
        return pl.multiple_of(g * (G * BS), G * BS)

    def score_finish(g, ss, buf, m_run):
        m_new, alpha = [], []
        for hh in all_heads:
            buf[hh] = ss[hh]
            m = m_run[hh]
            for t in range(G):
                tile_max = jnp.max(ss[hh][t * BS:(t + 1) * BS], axis=0, keepdims=True)
                m = jnp.maximum(m, tile_max + row_bias(sels[hh], g * G + t))
            m_new.append(m)
            alpha.append(jnp.exp2(m_run[hh] - m))
        return m_new, alpha

    def score_stage(g, buf, m_run):
        return score_finish(g, score_dots(group_start(g), G * BS), buf, m_run)

    def value_stage(g, buf, m_g, alpha, l, acc):
        start = group_start(g)
        l_new, pbs = [], []
        for hh in all_heads:
            p = [jnp.exp2(buf[hh, t * BS:(t + 1) * BS, :] + (row_bias(sels[hh], g * G + t) - m_g[hh]))
                 for t in range(G)]
            l_new.append(functools.reduce(
                jnp.add, [jnp.sum(pt, axis=0, keepdims=True) for pt in p], alpha[hh] * l[hh]))
            pbs.append(jnp.concatenate([pt.astype(BF16) for pt in p], axis=0))
        acc_new = [alpha[hh] * acc[hh] + value_dot(hh, start, G * BS, pbs[hh]) for hh in all_heads]
        return l_new, acc_new

    n_pairs = jnp.maximum((qi + 2 * G - 1) // (2 * G), 1)

    own_start = pl.multiple_of(qi * BS, BS)
    ss_own = score_dots(own_start, BS)
    ss_0 = score_dots(group_start(0), G * BS)
    sels = [select_blocks(qs[hh], hh) for hh in all_heads]
    ss_own = [s + causal_bias for s in ss_own]
    ms = [jnp.max(s, axis=0, keepdims=True) for s in ss_own]
    ps = [jnp.exp2(s - m) for s, m in zip(ss_own, ms)]
    l = [jnp.sum(p, axis=0, keepdims=True) for p in ps]
    m_a, alpha_a = score_finish(0, ss_0, sa_scr, ms)
    acc = [value_dot(hh, own_start, BS, ps[hh].astype(BF16)) for hh in all_heads]

    def pair(k, carry):
        m_a, alpha_a, l, acc = carry
        m_b, alpha_b = score_stage(2 * k + 1, sb_scr, m_a)
        l, acc = value_stage(2 * k, sa_scr, m_a, alpha_a, l, acc)
        m_n, alpha_n = score_stage(2 * k + 2, sa_scr, m_b)
        l, acc = value_stage(2 * k + 1, sb_scr, m_b, alpha_b, l, acc)
        return m_n, alpha_n, l, acc

    m_a, alpha_a, l, acc = lax.fori_loop(0, n_pairs - 1, pair, (m_a, alpha_a, l, acc))
    last = 2 * (n_pairs - 1)
    m_b, alpha_b = score_stage(last + 1, sb_scr, m_a)
    l, acc = value_stage(last, sa_scr, m_a, alpha_a, l, acc)
    l, acc = value_stage(last + 1, sb_scr, m_b, alpha_b, l, acc)
    for hh in all_heads:
        o_ref[:, hh * HEAD_DIM:(hh + 1) * HEAD_DIM] = (acc[hh] * (1.0 / l[hh])).T.astype(o_ref.dtype)


def _moba_attention(z, batch, seq_len):
    M = z.shape[1]
    BS = MOBA_BLOCK
    n_blocks = seq_len // BS
    H = ATTN_HEADS
    heads = 4
    group = 2
    assert seq_len % BS == 0 and n_blocks - 1 >= MOBA_TOPK and n_blocks % (2 * group) == 0
    hb = H // heads
    kern = functools.partial(_moba_kernel, n_blocks=n_blocks, group=group, heads=heads)
    return pl.pallas_call(
        kern,
        grid=(batch * hb, n_blocks),
        in_specs=[
            pl.BlockSpec((heads, BS, HEAD_DIM), lambda bh, qi: (SEG_AQ * hb + bh % hb, (bh // hb) * n_blocks + qi, 0)),
            pl.BlockSpec((heads, seq_len, HEAD_DIM), lambda bh, qi: (SEG_AK * hb + bh % hb, bh // hb, 0)),
            pl.BlockSpec((heads, seq_len, HEAD_DIM), lambda bh, qi: (SEG_AV * hb + bh % hb, bh // hb, 0)),
        ],
        out_specs=pl.BlockSpec((BS, heads * HEAD_DIM), lambda bh, qi: ((bh // hb) * n_blocks + qi, bh % hb)),
        out_shape=jax.ShapeDtypeStruct((M, ATTN_WIDTH), BF16),
        scratch_shapes=[pltpu.VMEM((heads, n_blocks, HEAD_DIM), F32),
                        pltpu.VMEM((heads, HEAD_DIM, seq_len), BF16),
                        pltpu.VMEM((heads, group * BS, BS), F32),
                        pltpu.VMEM((heads, group * BS, BS), F32)],
        compiler_params=pltpu.CompilerParams(
            dimension_semantics=("parallel", "arbitrary"), vmem_limit_bytes=VMEM_LIMIT_BYTES),
        name="moba_attention",
    )(z, z, z)


def _retention_kernel(lg_ref, q_ref, k_ref, v_ref, g_ref, gret_ref, o_ref, s_scr, *, chunk, n_chunks):
    C = chunk
    head = pl.program_id(0) % RET_HEADS
    lg = lg_ref[head]

    @pl.when(pl.program_id(1) == 0)
    def _():
        s_scr[...] = jnp.zeros_like(s_scr)

    row = lax.broadcasted_iota(jnp.int32, (C, C), 0)
    col = lax.broadcasted_iota(jnp.int32, (C, C), 1)
    rel = (row - col).astype(F32)
    decay = jnp.where(rel >= 0, jnp.exp(lg * jnp.maximum(rel, 0.0)), 0.0)
    pos = lax.broadcasted_iota(jnp.int32, (C, HEAD_DIM), 0).astype(F32)
    k_decay = jnp.exp(lg * (C - 1.0 - pos))
    q_decay = jnp.exp(lg * (pos + 1.0))
    chunk_decay = jnp.exp(jnp.full((1, HEAD_DIM), lg * C, F32))
    gret = gret_ref[...]

    S = s_scr[...]
    for c in range(n_chunks):
        sl = slice(c * C, (c + 1) * C)
        q = q_ref[0, sl, :]
        k = k_ref[0, sl, :]
        v = v_ref[0, sl, :]
        s = lax.dot_general(q, k, _NT, preferred_element_type=F32) * decay
        intra = jnp.dot(s.astype(BF16), v, preferred_element_type=F32)
        cross = jnp.dot((q.astype(F32) * q_decay).astype(BF16), S.astype(BF16),
                        preferred_element_type=F32)
        r = intra + cross
        U = lax.dot_general((k.astype(F32) * k_decay).astype(BF16), v, _TN, preferred_element_type=F32)
        S = chunk_decay * S + U
        r = r * _rms_scale(r) * gret
        gt = g_ref[0, sl, :].astype(F32)
        o_ref[sl, :] = (gt * _sigmoid(gt) * r).astype(o_ref.dtype)
    s_scr[...] = S


def _retention(z, log_decay, g_ret, batch, seq_len):
    M = z.shape[1]
    H = RET_HEADS
    chunk = _pick(seq_len, 256)
    tb = _pick(seq_len, 2048)
    t_blocks = seq_len // tb

    def zspec(seg):
        return pl.BlockSpec((1, tb, HEAD_DIM), lambda bh, t: (seg * H + bh % H, (bh // H) * t_blocks + t, 0))

    kern = functools.partial(_retention_kernel, chunk=chunk, n_chunks=tb // chunk)
    return pl.pallas_call(
        kern,
        grid=(batch * H, t_blocks),
        in_specs=[
            pl.BlockSpec(memory_space=pltpu.SMEM),
            zspec(SEG_RQ), zspec(SEG_RK), zspec(SEG_RV), zspec(SEG_RG),
            pl.BlockSpec((1, HEAD_DIM), lambda bh, t: (0, bh % H)),
        ],
        out_specs=pl.BlockSpec((tb, HEAD_DIM), lambda bh, t: ((bh // H) * t_blocks + t, bh % H)),
        out_shape=jax.ShapeDtypeStruct((M, RET_WIDTH), BF16),
        scratch_shapes=[pltpu.VMEM((HEAD_DIM, HEAD_DIM), F32)],
        compiler_params=pltpu.CompilerParams(
            dimension_semantics=("parallel", "arbitrary"), vmem_limit_bytes=VMEM_LIMIT_BYTES),
        name="retention",
    )(log_decay, z, z, z, z, g_ret)


def _oproj_kernel(a_ref, r_ref, wa_ref, wr_ref, x_ref, g_ref, x1_ref, h_ref):
    mix = (jnp.dot(a_ref[...], wa_ref[...], preferred_element_type=F32)
           + jnp.dot(r_ref[...], wr_ref[...], preferred_element_type=F32))
    x1 = x_ref[...] + mix
    x1_ref[...] = x1
    h_ref[...] = (x1 * _rms_scale(x1) * g_ref[...]).astype(h_ref.dtype)


def _out_projection(a_out, r_out, w_o, x, g_ffn):
    M, D = x.shape
    tm = _pick(M, 512)
    assert ATTN_WIDTH == RET_WIDTH
    return pl.pallas_call(
        _oproj_kernel,
        grid=(M // tm,),
        in_specs=[
            pl.BlockSpec((tm, ATTN_WIDTH), lambda i: (i, 0)),
            pl.BlockSpec((tm, RET_WIDTH), lambda i: (i, 0)),
            pl.BlockSpec((ATTN_WIDTH, D), lambda i: (0, 0)),
            pl.BlockSpec((RET_WIDTH, D), lambda i: (1, 0)),
            pl.BlockSpec((tm, D), lambda i: (i, 0)),
            pl.BlockSpec((1, D), lambda i: (0, 0)),
        ],
        out_specs=[pl.BlockSpec((tm, D), lambda i: (i, 0)), pl.BlockSpec((tm, D), lambda i: (i, 0))],
        out_shape=[jax.ShapeDtypeStruct((M, D), F32), jax.ShapeDtypeStruct((M, D), BF16)],
        compiler_params=pltpu.CompilerParams(
            dimension_semantics=("parallel",), vmem_limit_bytes=VMEM_LIMIT_BYTES),
        name="out_projection",
    )(a_out, r_out, w_o, w_o, x, g_ffn)


def _ffn_kernel(h_ref, x1_ref, wg_ref, wu_ref, wd_ref, g_ref, x2_ref, h3_ref):
    f = pl.program_id(1)

    @pl.when(f == 0)
    def _():
        x2_ref[...] = x1_ref[...]

    h = h_ref[...]
    gate = jnp.dot(h, wg_ref[...], preferred_element_type=F32)
    up = jnp.dot(h, wu_ref[...], preferred_element_type=F32)
    act = (gate * _sigmoid(gate) * up).astype(BF16)
    x2_ref[...] += jnp.dot(act, wd_ref[...], preferred_element_type=F32)

    @pl.when(f == pl.num_programs(1) - 1)
    def _():
        x2 = x2_ref[...]
        h3_ref[...] = (x2 * _rms_scale(x2) * g_ref[...]).astype(h3_ref.dtype)


def _ffn(h2, x1, w_gate, w_up, w_down, g_ple):
    M, D = x1.shape
    F = w_gate.shape[1]
    tm = _pick(M, 512)
    tf = _pick(F, 512)
    return pl.pallas_call(
        _ffn_kernel,
        grid=(M // tm, F // tf),
        in_specs=[
            pl.BlockSpec((tm, D), lambda i, f: (i, 0)),
            pl.BlockSpec((tm, D), lambda i, f: (i, 0)),
            pl.BlockSpec((D, tf), lambda i, f: (0, f)),
            pl.BlockSpec((D, tf), lambda i, f: (0, f)),
            pl.BlockSpec((tf, D), lambda i, f: (f, 0)),
            pl.BlockSpec((1, D), lambda i, f: (0, 0)),
        ],
        out_specs=[pl.BlockSpec((tm, D), lambda i, f: (i, 0)), pl.BlockSpec((tm, D), lambda i, f: (i, 0))],
        out_shape=[jax.ShapeDtypeStruct((M, D), F32), jax.ShapeDtypeStruct((M, D), BF16)],
        compiler_params=pltpu.CompilerParams(
            dimension_semantics=("parallel", "arbitrary"), vmem_limit_bytes=VMEM_LIMIT_BYTES),
        name="swiglu_ffn",
    )(h2, x1, w_gate, w_up, w_down, g_ple)


def _ple_kernel(h_ref, p_ref, x2_ref, wg_ref, b_ref, wp_ref, o_ref):
    gate = _sigmoid(jnp.dot(h_ref[...], wg_ref[...], preferred_element_type=F32) + b_ref[...])
    proj = jnp.dot(p_ref[...].astype(BF16), wp_ref[...], preferred_element_type=F32)
    o_ref[...] = x2_ref[...] + proj * gate


def _ple(h3, p, x2, w_gate, b_gate, w_proj):
    M, D = x2.shape
    P = p.shape[1]
    tm = _pick(M, 512)
    return pl.pallas_call(
        _ple_kernel,
        grid=(M // tm,),
        in_specs=[
            pl.BlockSpec((tm, D), lambda i: (i, 0)),
            pl.BlockSpec((tm, P), lambda i: (i, 0)),
            pl.BlockSpec((tm, D), lambda i: (i, 0)),
            pl.BlockSpec((D, D), lambda i: (0, 0)),
            pl.BlockSpec((1, D), lambda i: (0, 0)),
            pl.BlockSpec((P, D), lambda i: (0, 0)),
        ],
        out_specs=pl.BlockSpec((tm, D), lambda i: (i, 0)),
        out_shape=jax.ShapeDtypeStruct((M, D), F32),
        compiler_params=pltpu.CompilerParams(
            dimension_semantics=("parallel",), vmem_limit_bytes=VMEM_LIMIT_BYTES),
        name="gated_ple",
    )(h3, p, x2, w_gate, b_gate, w_proj)


def _rope_tables(seq_len):
    pos = jnp.arange(seq_len, dtype=F32)
    inv_attn = jnp.power(ROPE_THETA, -jnp.arange(0, HEAD_DIM, 2, dtype=F32) / HEAD_DIM)
    inv_ret = jnp.power(ROPE_THETA, -jnp.linspace(0.0, 1.0, HEAD_DIM // 2, dtype=F32))

    def table(inv_freq):
        ang = pos[:, None] * inv_freq[None, :]
        c, s = jnp.cos(ang), jnp.sin(ang)
        return jnp.concatenate([c, c], axis=-1), jnp.concatenate([-s, s], axis=-1)

    ca, sa = table(inv_attn)
    cr, sr = table(inv_ret)
    return jnp.stack([ca, cr]), jnp.stack([sa, sr])


def kernel(x, p, g_mix, w_in, q_norm, k_norm, g_ret, w_o, g_ffn, w_gate, w_up, w_down, g_ple,
           w_ple_gate, b_ple_gate, w_ple_proj):
    B, T, D = x.shape
    depth = w_in.shape[0]
    M = B * T
    assert w_in.shape[2] == 3 * ATTN_WIDTH + 4 * RET_WIDTH and ATTN_WIDTH == RET_WIDTH

    cos_tab, sin_tab = _rope_tables(T)
    log_decay = jnp.log(1.0 - jnp.power(2.0, -5.0 - jnp.arange(RET_HEADS, dtype=F32)))
    ones = jnp.ones((HEAD_DIM,), F32)

    xf = x.reshape(M, D)
    for i in range(depth):
        gains = jnp.stack([q_norm[i] * (LOG2_E * HEAD_DIM ** -0.5), k_norm[i], ones, ones,
                           ones * (HEAD_DIM ** -0.5), ones, ones])
        z = _in_projection(xf, g_mix[i][None], w_in[i].astype(BF16), gains[:, None, :],
                           cos_tab, sin_tab, T)
        a_out = _moba_attention(z, B, T)
        r_out = _retention(z, log_decay, g_ret[i][None], B, T)
        x1, h2 = _out_projection(a_out, r_out, w_o[i].astype(BF16), xf, g_ffn[i][None])
        x2, h3 = _ffn(h2, x1, w_gate[i].astype(BF16), w_up[i].astype(BF16), w_down[i].astype(BF16),
                      g_ple[i][None])
        xf = _ple(h3, p[i].reshape(M, -1), x2, w_ple_gate[i].astype(BF16), b_ple_gate[i][None],
                  w_ple_proj[i].astype(BF16))
    return xf.reshape(B, T, D)
```

```python
import functools

import jax
import jax.numpy as jnp
from jax import lax
from jax.experimental import pallas as pl
from jax.experimental.pallas import tpu as pltpu

HEAD_DIM = 128
ATTN_HEADS = 8
RET_HEADS = 8
ATTN_WIDTH = ATTN_HEADS * HEAD_DIM
RET_WIDTH = RET_HEADS * HEAD_DIM
MOBA_BLOCK = 256
MOBA_TOPK = 3
ROPE_THETA = 10000.0
RMS_EPS = 1e-6
NEG_INF = -1e30
LOG2_E = 1.4426950408889634

N_SEGMENTS = 7
SEG_AQ, SEG_AK, SEG_AV, SEG_RQ, SEG_RK, SEG_RV, SEG_RG = range(N_SEGMENTS)
ROPE_ATTN, ROPE_RET = 0, 1

LANES = 128
VMEM_LIMIT_BYTES = 56 * 1024 * 1024

F32 = jnp.float32
BF16 = jnp.bfloat16

_NT = (((1,), (1,)), ((), ()))
_TN = (((0,), (0,)), ((), ()))


def _sigmoid(x):
    return 1.0 / (1.0 + jnp.exp(-x))


def _rms_scale(x):
    return lax.rsqrt(jnp.mean(x * x, axis=-1, keepdims=True) + RMS_EPS)


def _pick(dim, pref):
    t = min(dim, pref)
    while dim % t:
        t //= 2
    return t


def _inproj_kernel(x_ref, gmix_ref, w_ref, gain_ref, cos_ref, sin_ref, z_ref, h_scr, *,
                   blocks_per_seg, slab):
    j = pl.program_id(1)
    seg = j // blocks_per_seg

    @pl.when(j == 0)
    def _():
        x = x_ref[...]
        h_scr[...] = (x * _rms_scale(x) * gmix_ref[...]).astype(h_scr.dtype)

    def step(norm, rope):
        heads_per_slab = slab // HEAD_DIM
        n_slabs = w_ref.shape[1] // slab
        ones_bf = jnp.ones((HEAD_DIM, HEAD_DIM), BF16)

        def slab_dot(s):
            return jnp.dot(h_scr[...], w_ref[:, s * slab:(s + 1) * slab], preferred_element_type=F32)

        acc_next = slab_dot(0)
        for s in range(n_slabs):
            acc = acc_next
            if s + 1 < n_slabs:
                acc_next = slab_dot(s + 1)
            for hh in range(heads_per_slab):
                y = acc[:, hh * HEAD_DIM:(hh + 1) * HEAD_DIM]
                out = y
                if rope:
                    y = y * gain_ref[0]
                    out = y * cos_ref[0] + pltpu.roll(y, HEAD_DIM // 2, 1) * sin_ref[0]
                if norm:
                    y0 = acc[:, hh * HEAD_DIM:(hh + 1) * HEAD_DIM]
                    sq = y0 * y0
                    hi = sq.astype(BF16)
                    lo = (sq - hi.astype(F32)).astype(BF16)
                    ssum = (jnp.dot(hi, ones_bf, preferred_element_type=F32)
                            + jnp.dot(lo, ones_bf, preferred_element_type=F32))
                    out = out * lax.rsqrt(ssum * (1.0 / HEAD_DIM) + RMS_EPS)
                z_ref[s * heads_per_slab + hh] = out.astype(z_ref.dtype)

    is_attn_qk = seg <= SEG_AK
    is_ret_qk = (seg == SEG_RQ) | (seg == SEG_RK)
    pl.when(is_attn_qk)(lambda: step(norm=True, rope=True))
    pl.when(is_ret_qk)(lambda: step(norm=False, rope=True))
    pl.when(jnp.logical_not(is_attn_qk | is_ret_qk))(lambda: step(norm=False, rope=False))


def _in_projection(x, g_mix, w_in, gains, cos_tab, sin_tab, seq_len):
    M, D = x.shape
    n_cols = w_in.shape[1]
    tm = _pick(seq_len, 1024)
    tn = _pick(ATTN_WIDTH, 1024)
    blocks_per_seg = ATTN_WIDTH // tn
    t_blocks = seq_len // tm

    def seg_of(j):
        return j // blocks_per_seg

    def rope_of(j):
        return jnp.where(seg_of(j) <= SEG_AV, ROPE_ATTN, ROPE_RET)

    kern = functools.partial(_inproj_kernel, blocks_per_seg=blocks_per_seg, slab=_pick(tn, 256))
    return pl.pallas_call(
        kern,
        grid=(M // tm, n_cols // tn),
        in_specs=[
            pl.BlockSpec((tm, D), lambda i, j: (i, 0)),
            pl.BlockSpec((1, D), lambda i, j: (0, 0)),
            pl.BlockSpec((D, tn), lambda i, j: (0, j)),
            pl.BlockSpec((1, 1, HEAD_DIM), lambda i, j: (seg_of(j), 0, 0)),
            pl.BlockSpec((1, tm, HEAD_DIM), lambda i, j: (rope_of(j), i % t_blocks, 0)),
            pl.BlockSpec((1, tm, HEAD_DIM), lambda i, j: (rope_of(j), i % t_blocks, 0)),
        ],
        out_specs=pl.BlockSpec((tn // HEAD_DIM, tm, HEAD_DIM), lambda i, j: (j, i, 0)),
        out_shape=jax.ShapeDtypeStruct((n_cols // HEAD_DIM, M, HEAD_DIM), BF16),
        scratch_shapes=[pltpu.VMEM((tm, D), BF16)],
        compiler_params=pltpu.CompilerParams(
            dimension_semantics=("parallel", "arbitrary"), vmem_limit_bytes=VMEM_LIMIT_BYTES),
        name="in_projection",
    )(x, g_mix, w_in, gains, cos_tab, sin_tab)


def _moba_kernel(q_ref, k_ref, v_ref, o_ref, kmean_scr, vt_scr, sa_scr, sb_scr, *, n_blocks, group, heads):
    BS = MOBA_BLOCK
    G = group
    qi = pl.program_id(1)

    @pl.when(qi == 0)
    def _():
        for hh in range(heads):
            for n in range(n_blocks):
                kb = k_ref[hh, n * BS:(n + 1) * BS, :].astype(F32)
                kmean_scr[hh, n:n + 1, :] = jnp.mean(kb, axis=0, keepdims=True)

            def fill(c, carry, hh=hh):
                start = pl.multiple_of(c * (G * BS), G * BS)
                vt_scr[hh, :, pl.ds(start, G * BS)] = (
                    v_ref[hh, pl.ds(start, G * BS), :].astype(F32).T.astype(vt_scr.dtype))
                return carry

            lax.fori_loop(0, n_blocks // G, fill, 0)

    kpos = lax.broadcasted_iota(jnp.int32, (BS, BS), 0)
    qpos = lax.broadcasted_iota(jnp.int32, (BS, BS), 1)
    causal_bias = jnp.where(kpos <= qpos, 0.0, NEG_INF)

    def select_blocks(q, hh):
        gate = lax.dot_general(kmean_scr[hh].astype(BF16), q, _NT, preferred_element_type=F32)
        blk = lax.broadcasted_iota(jnp.int32, gate.shape, 0)
        blk_f = blk.astype(F32)
        gate = jnp.where(blk < qi, gate, NEG_INF)
        sel = []
        for r in range(MOBA_TOPK):
            mx = jnp.max(gate, axis=0, keepdims=True)
            pick = jnp.min(jnp.where(gate == mx, blk_f, float(n_blocks)), axis=0, keepdims=True)
            sel.append(jnp.where(r < qi, pick, -1.0))
            gate = jnp.where(blk_f == pick, -jnp.inf, gate)
        return sel

    def row_bias(sel, n):
        n_f = lax.convert_element_type(n, F32)
        allowed = (sel[0] == n_f) | (sel[1] == n_f) | (sel[2] == n_f)
        return jnp.where(allowed, 0.0, NEG_INF)

    qs = [q_ref[hh] for hh in range(heads)]
    qts = [q.astype(F32).T.astype(BF16) for q in qs]
    all_heads = range(heads)

    def score_dots(start, size):
        return [jnp.dot(k_ref[hh, pl.ds(start, size), :], qts[hh], preferred_element_type=F32)
                for hh in all_heads]

    def value_dot(hh, start, size, pb):
        return jnp.dot(vt_scr[hh, :, pl.ds(start, size)], pb, preferred_element_type=F32)

    def group_start(g):
        return pl.multiple_of(g * (G * BS), G * BS)

    def score_finish(g, ss, buf, m_run):
        m_new, alpha = [], []
        for hh in all_heads:
            buf[hh] = ss[hh]
            m = m_run[hh]
            for t in range(G):
                tile_max = jnp.max(ss[hh][t * BS:(t + 1) * BS], axis=0, keepdims=True)
                m = jnp.maximum(m, tile_max + row_bias(sels[hh], g * G + t))
            m_new.append(m)
            alpha.append(jnp.exp2(m_run[hh] - m))
        return m_new, alpha

    def score_stage(g, buf, m_run):
        return score_finish(g, score_dots(group_start(g), G * BS), buf, m_run)

    def value_stage(g, buf, m_g, alpha, l, acc):
        start = group_start(g)
        l_new, pbs = [], []
        for hh in all_heads:
            p = [jnp.exp2(buf[hh, t * BS:(t + 1) * BS, :] + (row_bias(sels[hh], g * G + t) - m_g[hh]))
                 for t in range(G)]
            l_new.append(functools.reduce(
                jnp.add, [jnp.sum(pt, axis=0, keepdims=True) for pt in p], alpha[hh] * l[hh]))
            pbs.append(jnp.concatenate([pt.astype(BF16) for pt in p], axis=0))
        acc_new = [alpha[hh] * acc[hh] + value_dot(hh, start, G * BS, pbs[hh]) for hh in all_heads]
        return l_new, acc_new

    n_pairs = jnp.maximum((qi + 2 * G - 1) // (2 * G), 1)

    own_start = pl.multiple_of(qi * BS, BS)
    ss_own = score_dots(own_start, BS)
    ss_0 = score_dots(group_start(0), G * BS)
    sels = [select_blocks(qs[hh], hh) for hh in all_heads]
    ss_own = [s + causal_bias for s in ss_own]
    ms = [jnp.max(s, axis=0, keepdims=True) for s in ss_own]
    ps = [jnp.exp2(s - m) for s, m in zip(ss_own, ms)]
    l = [jnp.sum(p, axis=0, keepdims=True) for p in ps]
    m_a, alpha_a = score_finish(0, ss_0, sa_scr, ms)
    acc = [value_dot(hh, own_start, BS, ps[hh].astype(BF16)) for hh in all_heads]

    def pair(k, carry):
        m_a, alpha_a, l, acc = carry
        m_b, alpha_b = score_stage(2 * k + 1, sb_scr, m_a)
        l, acc = value_stage(2 * k, sa_scr, m_a, alpha_a, l, acc)
        m_n, alpha_n = score_stage(2 * k + 2, sa_scr, m_b)
        l, acc = value_stage(2 * k + 1, sb_scr, m_b, alpha_b, l, acc)
        return m_n, alpha_n, l, acc

    m_a, alpha_a, l, acc = lax.fori_loop(0, n_pairs - 1, pair, (m_a, alpha_a, l, acc))
    last = 2 * (n_pairs - 1)
    m_b, alpha_b = score_stage(last + 1, sb_scr, m_a)
    l, acc = value_stage(last, sa_scr, m_a, alpha_a, l, acc)
    l, acc = value_stage(last + 1, sb_scr, m_b, alpha_b, l, acc)
    for hh in all_heads:
        o_ref[:, hh * HEAD_DIM:(hh + 1) * HEAD_DIM] = (acc[hh] * (1.0 / l[hh])).T.astype(o_ref.dtype)


def _moba_attention(z, batch, seq_len):
    M = z.shape[1]
    BS = MOBA_BLOCK
    n_blocks = seq_len // BS
    H = ATTN_HEADS
    heads = 4
    group = 2
    assert seq_len % BS == 0 and n_blocks - 1 >= MOBA_TOPK and n_blocks % (2 * group) == 0
    hb = H // heads
    kern = functools.partial(_moba_kernel, n_blocks=n_blocks, group=group, heads=heads)
    return pl.pallas_call(
        kern,
        grid=(batch * hb, n_blocks),
        in_specs=[
            pl.BlockSpec((heads, BS, HEAD_DIM), lambda bh, qi: (SEG_AQ * hb + bh % hb, (bh // hb) * n_blocks + qi, 0)),
            pl.BlockSpec((heads, seq_len, HEAD_DIM), lambda bh, qi: (SEG_AK * hb + bh % hb, bh // hb, 0)),
            pl.BlockSpec((heads, seq_len, HEAD_DIM), lambda bh, qi: (SEG_AV * hb + bh % hb, bh // hb, 0)),
        ],
        out_specs=pl.BlockSpec((BS, heads * HEAD_DIM), lambda bh, qi: ((bh // hb) * n_blocks + qi, bh % hb)),
        out_shape=jax.ShapeDtypeStruct((M, ATTN_WIDTH), BF16),
        scratch_shapes=[pltpu.VMEM((heads, n_blocks, HEAD_DIM), F32),
                        pltpu.VMEM((heads, HEAD_DIM, seq_len), BF16),
                        pltpu.VMEM((heads, group * BS, BS), F32),
                        pltpu.VMEM((heads, group * BS, BS), F32)],
        compiler_params=pltpu.CompilerParams(
            dimension_semantics=("parallel", "arbitrary"), vmem_limit_bytes=VMEM_LIMIT_BYTES),
        name="moba_attention",
    )(z, z, z)


def _retention_kernel(lg_ref, q_ref, k_ref, v_ref, g_ref, gret_ref, o_ref, s_scr, *, chunk, n_chunks):
    C = chunk
    head = pl.program_id(0) % RET_HEADS
    lg = lg_ref[head]

    @pl.when(pl.program_id(1) == 0)
    def _():
        s_scr[...] = jnp.zeros_like(s_scr)

    row = lax.broadcasted_iota(jnp.int32, (C, C), 0)
    col = lax.broadcasted_iota(jnp.int32, (C, C), 1)
    rel = (row - col).astype(F32)
    decay = jnp.where(rel >= 0, jnp.exp(lg * jnp.maximum(rel, 0.0)), 0.0)
    pos = lax.broadcasted_iota(jnp.int32, (C, HEAD_DIM), 0).astype(F32)
    k_decay = jnp.exp(lg * (C - 1.0 - pos))
    q_decay = jnp.exp(lg * (pos + 1.0))
    chunk_decay = jnp.exp(jnp.full((1, HEAD_DIM), lg * C, F32))
    gret = gret_ref[...]

    S = s_scr[...]
    for c in range(n_chunks):
        sl = slice(c * C, (c + 1) * C)
        q = q_ref[0, sl, :]
        k = k_ref[0, sl, :]
        v = v_ref[0, sl, :]
        s = lax.dot_general(q, k, _NT, preferred_element_type=F32) * decay
        intra = jnp.dot(s.astype(BF16), v, preferred_element_type=F32)
        cross = jnp.dot((q.astype(F32) * q_decay).astype(BF16), S.astype(BF16),
                        preferred_element_type=F32)
        r = intra + cross
        U = lax.dot_general((k.astype(F32) * k_decay).astype(BF16), v, _TN, preferred_element_type=F32)
        S = chunk_decay * S + U
        r = r * _rms_scale(r) * gret
        gt = g_ref[0, sl, :].astype(F32)
        o_ref[sl, :] = (gt * _sigmoid(gt) * r).astype(o_ref.dtype)
    s_scr[...] = S


def _retention(z, log_decay, g_ret, batch, seq_len):
    M = z.shape[1]
    H = RET_HEADS
    chunk = _pick(seq_len, 256)
    tb = _pick(seq_len, 4096)
    t_blocks = seq_len // tb

    def zspec(seg):
        return pl.BlockSpec((1, tb, HEAD_DIM), lambda bh, t: (seg * H + bh % H, (bh // H) * t_blocks + t, 0))

    kern = functools.partial(_retention_kernel, chunk=chunk, n_chunks=tb // chunk)
    return pl.pallas_call(
        kern,
        grid=(batch * H, t_blocks),
        in_specs=[
            pl.BlockSpec(memory_space=pltpu.SMEM),
            zspec(SEG_RQ), zspec(SEG_RK), zspec(SEG_RV), zspec(SEG_RG),
            pl.BlockSpec((1, HEAD_DIM), lambda bh, t: (0, bh % H)),
        ],
        out_specs=pl.BlockSpec((tb, HEAD_DIM), lambda bh, t: ((bh // H) * t_blocks + t, bh % H)),
        out_shape=jax.ShapeDtypeStruct((M, RET_WIDTH), BF16),
        scratch_shapes=[pltpu.VMEM((HEAD_DIM, HEAD_DIM), F32)],
        compiler_params=pltpu.CompilerParams(
            dimension_semantics=("parallel", "arbitrary"), vmem_limit_bytes=VMEM_LIMIT_BYTES),
        name="retention",
    )(log_decay, z, z, z, z, g_ret)


def _oproj_kernel(a_ref, r_ref, wa_ref, wr_ref, x_ref, g_ref, x1_ref, h_ref):
    mix = (jnp.dot(a_ref[...], wa_ref[...], preferred_element_type=F32)
           + jnp.dot(r_ref[...], wr_ref[...], preferred_element_type=F32))
    x1 = x_ref[...] + mix
    x1_ref[...] = x1
    h_ref[...] = (x1 * _rms_scale(x1) * g_ref[...]).astype(h_ref.dtype)


def _out_projection(a_out, r_out, w_o, x, g_ffn):
    M, D = x.shape
    tm = _pick(M, 512)
    assert ATTN_WIDTH == RET_WIDTH
    return pl.pallas_call(
        _oproj_kernel,
        grid=(M // tm,),
        in_specs=[
            pl.BlockSpec((tm, ATTN_WIDTH), lambda i: (i, 0)),
            pl.BlockSpec((tm, RET_WIDTH), lambda i: (i, 0)),
            pl.BlockSpec((ATTN_WIDTH, D), lambda i: (0, 0)),
            pl.BlockSpec((RET_WIDTH, D), lambda i: (1, 0)),
            pl.BlockSpec((tm, D), lambda i: (i, 0)),
            pl.BlockSpec((1, D), lambda i: (0, 0)),
        ],
        out_specs=[pl.BlockSpec((tm, D), lambda i: (i, 0)), pl.BlockSpec((tm, D), lambda i: (i, 0))],
        out_shape=[jax.ShapeDtypeStruct((M, D), F32), jax.ShapeDtypeStruct((M, D), BF16)],
        compiler_params=pltpu.CompilerParams(
            dimension_semantics=("parallel",), vmem_limit_bytes=VMEM_LIMIT_BYTES),
        name="out_projection",
    )(a_out, r_out, w_o, w_o, x, g_ffn)


def _ffn_kernel(h_ref, wg_ref, wu_ref, wd_ref, y_ref):
    @pl.when(pl.program_id(1) == 0)
    def _():
        y_ref[...] = jnp.zeros_like(y_ref)

    h = h_ref[...]
    gate = jnp.dot(h, wg_ref[...], preferred_element_type=F32)
    up = jnp.dot(h, wu_ref[...], preferred_element_type=F32)
    act = (gate * _sigmoid(gate) * up).astype(BF16)
    y_ref[...] += jnp.dot(act, wd_ref[...], preferred_element_type=F32)


def _ffn(h2, w_gate, w_up, w_down):
    M, D = h2.shape
    F = w_gate.shape[1]
    tm = _pick(M, 1024)
    tf = _pick(F, 512)
    return pl.pallas_call(
        _ffn_kernel,
        grid=(M // tm, F // tf),
        in_specs=[
            pl.BlockSpec((tm, D), lambda i, f: (i, 0)),
            pl.BlockSpec((D, tf), lambda i, f: (0, f)),
            pl.BlockSpec((D, tf), lambda i, f: (0, f)),
            pl.BlockSpec((tf, D), lambda i, f: (f, 0)),
        ],
        out_specs=pl.BlockSpec((tm, D), lambda i, f: (i, 0)),
        out_shape=jax.ShapeDtypeStruct((M, D), F32),
        compiler_params=pltpu.CompilerParams(
            dimension_semantics=("parallel", "arbitrary"), vmem_limit_bytes=VMEM_LIMIT_BYTES),
        name="swiglu_ffn",
    )(h2, w_gate, w_up, w_down)


def _ple_kernel(x1_ref, y_ref, p_ref, g_ref, wg_ref, b_ref, wp_ref, o_ref):
    x2 = x1_ref[...] + y_ref[...]
    h = (x2 * _rms_scale(x2) * g_ref[...]).astype(BF16)
    gate = _sigmoid(jnp.dot(h, wg_ref[...], preferred_element_type=F32) + b_ref[...])
    proj = jnp.dot(p_ref[...].astype(BF16), wp_ref[...], preferred_element_type=F32)
    o_ref[...] = x2 + proj * gate


def _ple(x1, y, p, g_ple, w_gate, b_gate, w_proj):
    M, D = x1.shape
    P = p.shape[1]
    tm = _pick(M, 512)
    return pl.pallas_call(
        _ple_kernel,
        grid=(M // tm,),
        in_specs=[
            pl.BlockSpec((tm, D), lambda i: (i, 0)),
            pl.BlockSpec((tm, D), lambda i: (i, 0)),
            pl.BlockSpec((tm, P), lambda i: (i, 0)),
            pl.BlockSpec((1, D), lambda i: (0, 0)),
            pl.BlockSpec((D, D), lambda i: (0, 0)),
            pl.BlockSpec((1, D), lambda i: (0, 0)),
            pl.BlockSpec((P, D), lambda i: (0, 0)),
        ],
        out_specs=pl.BlockSpec((tm, D), lambda i: (i, 0)),
        out_shape=jax.ShapeDtypeStruct((M, D), F32),
        compiler_params=pltpu.CompilerParams(
            dimension_semantics=("parallel",), vmem_limit_bytes=VMEM_LIMIT_BYTES),
        name="gated_ple",
    )(x1, y, p, g_ple, w_gate, b_gate, w_proj)


def _rope_tables(seq_len):
    pos = jnp.arange(seq_len, dtype=F32)
    inv_attn = jnp.power(ROPE_THETA, -jnp.arange(0, HEAD_DIM, 2, dtype=F32) / HEAD_DIM)
    inv_ret = jnp.power(ROPE_THETA, -jnp.linspace(0.0, 1.0, HEAD_DIM // 2, dtype=F32))

    def table(inv_freq):
        ang = pos[:, None] * inv_freq[None, :]
        c, s = jnp.cos(ang), jnp.sin(ang)
        return jnp.concatenate([c, c], axis=-1), jnp.concatenate([-s, s], axis=-1)

    ca, sa = table(inv_attn)
    cr, sr = table(inv_ret)
    return jnp.stack([ca, cr]), jnp.stack([sa, sr])


def kernel(x, p, g_mix, w_in, q_norm, k_norm, g_ret, w_o, g_ffn, w_gate, w_up, w_down, g_ple,
           w_ple_gate, b_ple_gate, w_ple_proj):
    B, T, D = x.shape
    depth = w_in.shape[0]
    M = B * T
    assert w_in.shape[2] == 3 * ATTN_WIDTH + 4 * RET_WIDTH and ATTN_WIDTH == RET_WIDTH

    cos_tab, sin_tab = _rope_tables(T)
    log_decay = jnp.log(1.0 - jnp.power(2.0, -5.0 - jnp.arange(RET_HEADS, dtype=F32)))
    ones = jnp.ones((HEAD_DIM,), F32)

    xf = x.reshape(M, D)
    for i in range(depth):
        gains = jnp.stack([q_norm[i] * (LOG2_E * HEAD_DIM ** -0.5), k_norm[i], ones, ones,
                           ones * (HEAD_DIM ** -0.5), ones, ones])
        z = _in_projection(xf, g_mix[i][None], w_in[i].astype(BF16), gains[:, None, :],
                           cos_tab, sin_tab, T)
        a_out = _moba_attention(z, B, T)
        r_out = _retention(z, log_decay, g_ret[i][None], B, T)
        x1, h2 = _out_projection(a_out, r_out, w_o[i].astype(BF16), xf, g_ffn[i][None])
        y = _ffn(h2, w_gate[i].astype(BF16), w_up[i].astype(BF16), w_down[i].astype(BF16))
        xf = _ple(x1, y, p[i].reshape(M, -1), g_ple[i][None], w_ple_gate[i].astype(BF16),
                  b_ple_gate[i][None], w_ple_proj[i].astype(BF16))
    return xf.reshape(B, T, D)
```

```python
import functools

import jax
import jax.numpy as jnp
from jax import lax
from jax.experimental import pallas as pl
from jax.experimental.pallas import tpu as pltpu

HEAD_DIM = 128
ATTN_HEADS = 8
RET_HEADS = 8
ATTN_WIDTH = ATTN_HEADS * HEAD_DIM
RET_WIDTH = RET_HEADS * HEAD_DIM
MOBA_BLOCK = 256
MOBA_TOPK = 3
ROPE_THETA = 10000.0
RMS_EPS = 1e-6
NEG_INF = -1e30
LOG2_E = 1.4426950408889634

N_SEGMENTS = 7
SEG_AQ, SEG_AK, SEG_AV, SEG_RQ, SEG_RK, SEG_RV, SEG_RG = range(N_SEGMENTS)
ROPE_ATTN, ROPE_RET = 0, 1

LANES = 128
VMEM_LIMIT_BYTES = 56 * 1024 * 1024

F32 = jnp.float32
BF16 = jnp.bfloat16

_NT = (((1,), (1,)), ((), ()))
_TN = (((0,), (0,)), ((), ()))


def _sigmoid(x):
    return 1.0 / (1.0 + jnp.exp(-x))


def _rms_scale(x):
    return lax.rsqrt(jnp.mean(x * x, axis=-1, keepdims=True) + RMS_EPS)


def _pick(dim, pref):
    t = min(dim, pref)
    while dim % t:
        t //= 2
    return t


def _inproj_kernel(x_ref, gmix_ref, w_ref, gain_ref, cos_ref, sin_ref, z_ref, h_scr, *,
                   blocks_per_seg, slab):
    j = pl.program_id(1)
    seg = j // blocks_per_seg

    @pl.when(j == 0)
    def _():
        x = x_ref[...]
        h_scr[...] = (x * _rms_scale(x) * gmix_ref[...]).astype(h_scr.dtype)

    def step(norm, rope):
        heads_per_slab = slab // HEAD_DIM
        n_slabs = w_ref.shape[1] // slab
        ones_bf = jnp.ones((HEAD_DIM, HEAD_DIM), BF16)

        def slab_dot(s):
            return jnp.dot(h_scr[...], w_ref[:, s * slab:(s + 1) * slab], preferred_element_type=F32)

        acc_next = slab_dot(0)
        for s in range(n_slabs):
            acc = acc_next
            if s + 1 < n_slabs:
                acc_next = slab_dot(s + 1)
            for hh in range(heads_per_slab):
                y = acc[:, hh * HEAD_DIM:(hh + 1) * HEAD_DIM]
                out = y
                if rope:
                    y = y * gain_ref[0]
                    out = y * cos_ref[0] + pltpu.roll(y, HEAD_DIM // 2, 1) * sin_ref[0]
                if norm:
                    y0 = acc[:, hh * HEAD_DIM:(hh + 1) * HEAD_DIM]
                    sq = y0 * y0
                    hi = sq.astype(BF16)
                    lo = (sq - hi.astype(F32)).astype(BF16)
                    ssum = (jnp.dot(hi, ones_bf, preferred_element_type=F32)
                            + jnp.dot(lo, ones_bf, preferred_element_type=F32))
                    out = out * lax.rsqrt(ssum * (1.0 / HEAD_DIM) + RMS_EPS)
                z_ref[s * heads_per_slab + hh] = out.astype(z_ref.dtype)

    is_attn_qk = seg <= SEG_AK
    is_ret_qk = (seg == SEG_RQ) | (seg == SEG_RK)
    pl.when(is_attn_qk)(lambda: step(norm=True, rope=True))
    pl.when(is_ret_qk)(lambda: step(norm=False, rope=True))
    pl.when(jnp.logical_not(is_attn_qk | is_ret_qk))(lambda: step(norm=False, rope=False))


def _in_projection(x, g_mix, w_in, gains, cos_tab, sin_tab, seq_len):
    M, D = x.shape
    n_cols = w_in.shape[1]
    tm = _pick(seq_len, 1024)
    tn = _pick(ATTN_WIDTH, 1024)
    blocks_per_seg = ATTN_WIDTH // tn
    t_blocks = seq_len // tm

    def seg_of(j):
        return j // blocks_per_seg

    def rope_of(j):
        return jnp.where(seg_of(j) <= SEG_AV, ROPE_ATTN, ROPE_RET)

    kern = functools.partial(_inproj_kernel, blocks_per_seg=blocks_per_seg, slab=_pick(tn, 256))
    return pl.pallas_call(
        kern,
        grid=(M // tm, n_cols // tn),
        in_specs=[
            pl.BlockSpec((tm, D), lambda i, j: (i, 0)),
            pl.BlockSpec((1, D), lambda i, j: (0, 0)),
            pl.BlockSpec((D, tn), lambda i, j: (0, j)),
            pl.BlockSpec((1, 1, HEAD_DIM), lambda i, j: (seg_of(j), 0, 0)),
            pl.BlockSpec((1, tm, HEAD_DIM), lambda i, j: (rope_of(j), i % t_blocks, 0)),
            pl.BlockSpec((1, tm, HEAD_DIM), lambda i, j: (rope_of(j), i % t_blocks, 0)),
        ],
        out_specs=pl.BlockSpec((tn // HEAD_DIM, tm, HEAD_DIM), lambda i, j: (j, i, 0)),
        out_shape=jax.ShapeDtypeStruct((n_cols // HEAD_DIM, M, HEAD_DIM), BF16),
        scratch_shapes=[pltpu.VMEM((tm, D), BF16)],
        compiler_params=pltpu.CompilerParams(
            dimension_semantics=("parallel", "arbitrary"), vmem_limit_bytes=VMEM_LIMIT_BYTES),
        name="in_projection",
    )(x, g_mix, w_in, gains, cos_tab, sin_tab)


def _moba_kernel(q_ref, k_ref, v_ref, o_ref, kmean_scr, vt_scr, sa_scr, sb_scr, *,
                 n_blocks, group, heads, q_blocks):
    BS = MOBA_BLOCK
    G = group
    BQ = q_blocks * BS
    step = pl.program_id(1)
    first_own = step * q_blocks

    @pl.when(step == 0)
    def _():
        for hh in range(heads):
            for n in range(n_blocks):
                kb = k_ref[hh, n * BS:(n + 1) * BS, :].astype(F32)
                kmean_scr[hh, n:n + 1, :] = jnp.mean(kb, axis=0, keepdims=True)

            def fill(c, carry, hh=hh):
                start = pl.multiple_of(c * (G * BS), G * BS)
                vt_scr[hh, :, pl.ds(start, G * BS)] = (
                    v_ref[hh, pl.ds(start, G * BS), :].astype(F32).T.astype(vt_scr.dtype))
                return carry

            lax.fori_loop(0, n_blocks // G, fill, 0)

    assert BS & (BS - 1) == 0
    q_blk = first_own + lax.shift_right_logical(lax.broadcasted_iota(jnp.int32, (1, BQ), 1),
                                                BS.bit_length() - 1)

    def select_blocks(qt, hh):
        gate = jnp.dot(kmean_scr[hh].astype(BF16), qt, preferred_element_type=F32)
        blk = lax.broadcasted_iota(jnp.int32, gate.shape, 0)
        blk_f = blk.astype(F32)
        gate = jnp.where(blk < q_blk, gate, NEG_INF)
        sel = []
        for r in range(MOBA_TOPK):
            mx = jnp.max(gate, axis=0, keepdims=True)
            pick = jnp.min(jnp.where(gate == mx, blk_f, float(n_blocks)), axis=0, keepdims=True)
            sel.append(jnp.where(r < q_blk, pick, -1.0))
            gate = jnp.where(blk_f == pick, -jnp.inf, gate)
        return sel

    def row_bias(sel, n):
        n_f = lax.convert_element_type(n, F32)
        allowed = (sel[0] == n_f) | (sel[1] == n_f) | (sel[2] == n_f)
        return jnp.where(allowed, 0.0, NEG_INF)

    def past_bias(sel, n):
        return jnp.where(n < first_own, row_bias(sel, n), NEG_INF)

    qts = [q_ref[hh].astype(F32).T.astype(BF16) for hh in range(heads)]
    all_heads = range(heads)

    def score_dots(start, size):
        return [jnp.dot(k_ref[hh, pl.ds(start, size), :], qts[hh], preferred_element_type=F32)
                for hh in all_heads]

    def value_dot(hh, start, size, pb):
        return jnp.dot(vt_scr[hh, :, pl.ds(start, size)], pb, preferred_element_type=F32)

    def group_start(g):
        return pl.multiple_of(g * (G * BS), G * BS)

    def score_finish(g, ss, buf, m_run):
        m_new, alpha = [], []
        for hh in all_heads:
            buf[hh] = ss[hh]
            m = m_run[hh]
            for t in range(G):
                tile_max = jnp.max(ss[hh][t * BS:(t + 1) * BS], axis=0, keepdims=True)
                m = jnp.maximum(m, tile_max + past_bias(sels[hh], g * G + t))
            m_new.append(m)
            alpha.append(jnp.exp2(m_run[hh] - m))
        return m_new, alpha

    def score_stage(g, buf, m_run):
        return score_finish(g, score_dots(group_start(g), G * BS), buf, m_run)

    def value_stage(g, buf, m_g, alpha, l, acc):
        start = group_start(g)
        l_new, pbs = [], []
        for hh in all_heads:
            p = [jnp.exp2(buf[hh, t * BS:(t + 1) * BS, :] + (past_bias(sels[hh], g * G + t) - m_g[hh]))
                 for t in range(G)]
            l_new.append(functools.reduce(
                jnp.add, [jnp.sum(pt, axis=0, keepdims=True) for pt in p], alpha[hh] * l[hh]))
            pbs.append(jnp.concatenate([pt.astype(BF16) for pt in p], axis=0))
        acc_new = [alpha[hh] * acc[hh] + value_dot(hh, start, G * BS, pbs[hh]) for hh in all_heads]
        return l_new, acc_new

    n_pairs = jnp.maximum((first_own + 2 * G - 1) // (2 * G), 1)

    sels = [select_blocks(qts[hh], hh) for hh in all_heads]

    own_start = pl.multiple_of(first_own * BS, BQ)
    ss_own = score_dots(own_start, BQ)
    ss_0 = score_dots(group_start(0), G * BS)
    kpos = lax.broadcasted_iota(jnp.int32, (BS, BQ), 0)
    qpos = lax.broadcasted_iota(jnp.int32, (BS, BQ), 1)
    own_tiles = []
    for hh in all_heads:
        tiles = []
        for t in range(q_blocks):
            same_block = (qpos >= t * BS) & (qpos < (t + 1) * BS)
            causal = jnp.where(kpos <= qpos - t * BS, 0.0, NEG_INF)
            later = jnp.where(qpos >= (t + 1) * BS,
                              jnp.broadcast_to(row_bias(sels[hh], first_own + t), (BS, BQ)), NEG_INF)
            tiles.append(ss_own[hh][t * BS:(t + 1) * BS] + jnp.where(same_block, causal, later))
        own_tiles.append(tiles)
    ms = [functools.reduce(jnp.maximum, [jnp.max(s, axis=0, keepdims=True) for s in tiles])
          for tiles in own_tiles]
    ps = [[jnp.exp2(s - ms[hh]) for s in own_tiles[hh]] for hh in all_heads]
    l = [functools.reduce(jnp.add, [jnp.sum(p, axis=0, keepdims=True) for p in ps[hh]]) for hh in all_heads]
    m_a, alpha_a = score_finish(0, ss_0, sa_scr, ms)
    acc = [value_dot(hh, own_start, BQ, jnp.concatenate([p.astype(BF16) for p in ps[hh]], axis=0))
           for hh in all_heads]

    def pair(k, carry):
        m_a, alpha_a, l, acc = carry
        m_b, alpha_b = score_stage(2 * k + 1, sb_scr, m_a)
        l, acc = value_stage(2 * k, sa_scr, m_a, alpha_a, l, acc)
        m_n, alpha_n = score_stage(2 * k + 2, sa_scr, m_b)
        l, acc = value_stage(2 * k + 1, sb_scr, m_b, alpha_b, l, acc)
        return m_n, alpha_n, l, acc

    m_a, alpha_a, l, acc = lax.fori_loop(0, n_pairs - 1, pair, (m_a, alpha_a, l, acc))
    last = 2 * (n_pairs - 1)
    m_b, alpha_b = score_stage(last + 1, sb_scr, m_a)
    l, acc = value_stage(last, sa_scr, m_a, alpha_a, l, acc)
    l, acc = value_stage(last + 1, sb_scr, m_b, alpha_b, l, acc)
    for hh in all_heads:
        o_ref[:, hh * HEAD_DIM:(hh + 1) * HEAD_DIM] = (acc[hh] * (1.0 / l[hh])).T.astype(o_ref.dtype)


def _moba_attention(z, batch, seq_len):
    M = z.shape[1]
    BS = MOBA_BLOCK
    n_blocks = seq_len // BS
    H = ATTN_HEADS
    heads = 4
    group = 2
    q_blocks = 2
    assert seq_len % BS == 0 and n_blocks - 1 >= MOBA_TOPK
    assert n_blocks % (2 * group) == 0 and n_blocks % q_blocks == 0
    hb = H // heads
    steps = n_blocks // q_blocks
    BQ = q_blocks * BS
    kern = functools.partial(_moba_kernel, n_blocks=n_blocks, group=group, heads=heads, q_blocks=q_blocks)
    return pl.pallas_call(
        kern,
        grid=(batch * hb, steps),
        in_specs=[
            pl.BlockSpec((heads, BQ, HEAD_DIM), lambda bh, s: (SEG_AQ * hb + bh % hb, (bh // hb) * steps + s, 0)),
            pl.BlockSpec((heads, seq_len, HEAD_DIM), lambda bh, s: (SEG_AK * hb + bh % hb, bh // hb, 0)),
            pl.BlockSpec((heads, seq_len, HEAD_DIM), lambda bh, s: (SEG_AV * hb + bh % hb, bh // hb, 0),
                         pipeline_mode=pl.Buffered(1)),
        ],
        out_specs=pl.BlockSpec((BQ, heads * HEAD_DIM), lambda bh, s: ((bh // hb) * steps + s, bh % hb)),
        out_shape=jax.ShapeDtypeStruct((M, ATTN_WIDTH), BF16),
        scratch_shapes=[pltpu.VMEM((heads, n_blocks, HEAD_DIM), F32),
                        pltpu.VMEM((heads, HEAD_DIM, seq_len), BF16),
                        pltpu.VMEM((heads, group * BS, BQ), F32),
                        pltpu.VMEM((heads, group * BS, BQ), F32)],
        compiler_params=pltpu.CompilerParams(
            dimension_semantics=("parallel", "arbitrary"), vmem_limit_bytes=VMEM_LIMIT_BYTES),
        name="moba_attention",
    )(z, z, z)


def _retention_kernel(lg_ref, q_ref, k_ref, v_ref, g_ref, gret_ref, o_ref, s_scr, *, chunk, n_chunks):
    C = chunk
    head = pl.program_id(0) % RET_HEADS
    lg = lg_ref[head]

    @pl.when(pl.program_id(1) == 0)
    def _():
        s_scr[...] = jnp.zeros_like(s_scr)

    row = lax.broadcasted_iota(jnp.int32, (C, C), 0)
    col = lax.broadcasted_iota(jnp.int32, (C, C), 1)
    rel = (row - col).astype(F32)
    decay = jnp.where(rel >= 0, jnp.exp(lg * jnp.maximum(rel, 0.0)), 0.0)
    pos = lax.broadcasted_iota(jnp.int32, (C, HEAD_DIM), 0).astype(F32)
    k_decay = jnp.exp(lg * (C - 1.0 - pos))
    q_decay = jnp.exp(lg * (pos + 1.0))
    chunk_decay = jnp.exp(jnp.full((1, HEAD_DIM), lg * C, F32))
    gret = gret_ref[...]

    S = s_scr[...]
    for c in range(n_chunks):
        sl = slice(c * C, (c + 1) * C)
        q = q_ref[0, sl, :]
        k = k_ref[0, sl, :]
        v = v_ref[0, sl, :]
        s = lax.dot_general(q, k, _NT, preferred_element_type=F32) * decay
        intra = jnp.dot(s.astype(BF16), v, preferred_element_type=F32)
        cross = jnp.dot((q.astype(F32) * q_decay).astype(BF16), S.astype(BF16),
                        preferred_element_type=F32)
        r = intra + cross
        U = lax.dot_general((k.astype(F32) * k_decay).astype(BF16), v, _TN, preferred_element_type=F32)
        S = chunk_decay * S + U
        r = r * _rms_scale(r) * gret
        gt = g_ref[0, sl, :].astype(F32)
        o_ref[sl, :] = (gt * _sigmoid(gt) * r).astype(o_ref.dtype)
    s_scr[...] = S


def _retention(z, log_decay, g_ret, batch, seq_len):
    M = z.shape[1]
    H = RET_HEADS
    chunk = _pick(seq_len, 256)
    tb = _pick(seq_len, 4096)
    t_blocks = seq_len // tb

    def zspec(seg):
        return pl.BlockSpec((1, tb, HEAD_DIM), lambda bh, t: (seg * H + bh % H, (bh // H) * t_blocks + t, 0))

    kern = functools.partial(_retention_kernel, chunk=chunk, n_chunks=tb // chunk)
    return pl.pallas_call(
        kern,
        grid=(batch * H, t_blocks),
        in_specs=[
            pl.BlockSpec(memory_space=pltpu.SMEM),
            zspec(SEG_RQ), zspec(SEG_RK), zspec(SEG_RV), zspec(SEG_RG),
            pl.BlockSpec((1, HEAD_DIM), lambda bh, t: (0, bh % H)),
        ],
        out_specs=pl.BlockSpec((tb, HEAD_DIM), lambda bh, t: ((bh // H) * t_blocks + t, bh % H)),
        out_shape=jax.ShapeDtypeStruct((M, RET_WIDTH), BF16),
        scratch_shapes=[pltpu.VMEM((HEAD_DIM, HEAD_DIM), F32)],
        compiler_params=pltpu.CompilerParams(
            dimension_semantics=("parallel", "arbitrary"), vmem_limit_bytes=VMEM_LIMIT_BYTES),
        name="retention",
    )(log_decay, z, z, z, z, g_ret)


def _oproj_kernel(a_ref, r_ref, wa_ref, wr_ref, x_ref, g_ref, x1_ref, h_ref):
    mix = (jnp.dot(a_ref[...], wa_ref[...], preferred_element_type=F32)
           + jnp.dot(r_ref[...], wr_ref[...], preferred_element_type=F32))
    x1 = x_ref[...] + mix
    x1_ref[...] = x1
    h_ref[...] = (x1 * _rms_scale(x1) * g_ref[...]).astype(h_ref.dtype)


def _out_projection(a_out, r_out, w_o, x, g_ffn):
    M, D = x.shape
    tm = _pick(M, 512)
    assert ATTN_WIDTH == RET_WIDTH
    return pl.pallas_call(
        _oproj_kernel,
        grid=(M // tm,),
        in_specs=[
            pl.BlockSpec((tm, ATTN_WIDTH), lambda i: (i, 0)),
            pl.BlockSpec((tm, RET_WIDTH), lambda i: (i, 0)),
            pl.BlockSpec((ATTN_WIDTH, D), lambda i: (0, 0)),
            pl.BlockSpec((RET_WIDTH, D), lambda i: (1, 0)),
            pl.BlockSpec((tm, D), lambda i: (i, 0)),
            pl.BlockSpec((1, D), lambda i: (0, 0)),
        ],
        out_specs=[pl.BlockSpec((tm, D), lambda i: (i, 0)), pl.BlockSpec((tm, D), lambda i: (i, 0))],
        out_shape=[jax.ShapeDtypeStruct((M, D), F32), jax.ShapeDtypeStruct((M, D), BF16)],
        compiler_params=pltpu.CompilerParams(
            dimension_semantics=("parallel",), vmem_limit_bytes=VMEM_LIMIT_BYTES),
        name="out_projection",
    )(a_out, r_out, w_o, w_o, x, g_ffn)


def _ffn_kernel(h_ref, wg_ref, wu_ref, wd_ref, y_ref):
    @pl.when(pl.program_id(1) == 0)
    def _():
        y_ref[...] = jnp.zeros_like(y_ref)

    h = h_ref[...]
    gate = jnp.dot(h, wg_ref[...], preferred_element_type=F32)
    up = jnp.dot(h, wu_ref[...], preferred_element_type=F32)
    act = (gate * _sigmoid(gate) * up).astype(BF16)
    y_ref[...] += jnp.dot(act, wd_ref[...], preferred_element_type=F32)


def _ffn(h2, w_gate, w_up, w_down):
    M, D = h2.shape
    F = w_gate.shape[1]
    tm = _pick(M, 1024)
    tf = _pick(F, 512)
    return pl.pallas_call(
        _ffn_kernel,
        grid=(M // tm, F // tf),
        in_specs=[
            pl.BlockSpec((tm, D), lambda i, f: (i, 0)),
            pl.BlockSpec((D, tf), lambda i, f: (0, f)),
            pl.BlockSpec((D, tf), lambda i, f: (0, f)),
            pl.BlockSpec((tf, D), lambda i, f: (f, 0)),
        ],
        out_specs=pl.BlockSpec((tm, D), lambda i, f: (i, 0)),
        out_shape=jax.ShapeDtypeStruct((M, D), F32),
        compiler_params=pltpu.CompilerParams(
            dimension_semantics=("parallel", "arbitrary"), vmem_limit_bytes=VMEM_LIMIT_BYTES),
        name="swiglu_ffn",
    )(h2, w_gate, w_up, w_down)


def _ple_kernel(x1_ref, y_ref, p_ref, g_ref, wg_ref, b_ref, wp_ref, o_ref):
    x2 = x1_ref[...] + y_ref[...]
    h = (x2 * _rms_scale(x2) * g_ref[...]).astype(BF16)
    gate = _sigmoid(jnp.dot(h, wg_ref[...], preferred_element_type=F32) + b_ref[...])
    proj = jnp.dot(p_ref[...].astype(BF16), wp_ref[...], preferred_element_type=F32)
    o_ref[...] = x2 + proj * gate


def _ple(x1, y, p, g_ple, w_gate, b_gate, w_proj):
    M, D = x1.shape
    P = p.shape[1]
    tm = _pick(M, 512)
    return pl.pallas_call(
        _ple_kernel,
        grid=(M // tm,),
        in_specs=[
            pl.BlockSpec((tm, D), lambda i: (i, 0)),
            pl.BlockSpec((tm, D), lambda i: (i, 0)),
            pl.BlockSpec((tm, P), lambda i: (i, 0)),
            pl.BlockSpec((1, D), lambda i: (0, 0)),
            pl.BlockSpec((D, D), lambda i: (0, 0)),
            pl.BlockSpec((1, D), lambda i: (0, 0)),
            pl.BlockSpec((P, D), lambda i: (0, 0)),
        ],
        out_specs=pl.BlockSpec((tm, D), lambda i: (i, 0)),
        out_shape=jax.ShapeDtypeStruct((M, D), F32),
        compiler_params=pltpu.CompilerParams(
            dimension_semantics=("parallel",), vmem_limit_bytes=VMEM_LIMIT_BYTES),
        name="gated_ple",
    )(x1, y, p, g_ple, w_gate, b_gate, w_proj)


def _rope_tables(seq_len):
    pos = jnp.arange(seq_len, dtype=F32)
    inv_attn = jnp.power(ROPE_THETA, -jnp.arange(0, HEAD_DIM, 2, dtype=F32) / HEAD_DIM)
    inv_ret = jnp.power(ROPE_THETA, -jnp.linspace(0.0, 1.0, HEAD_DIM // 2, dtype=F32))

    def table(inv_freq):
        ang = pos[:, None] * inv_freq[None, :]
        c, s = jnp.cos(ang), jnp.sin(ang)
        return jnp.concatenate([c, c], axis=-1), jnp.concatenate([-s, s], axis=-1)

    ca, sa = table(inv_attn)
    cr, sr = table(inv_ret)
    return jnp.stack([ca, cr]), jnp.stack([sa, sr])


def kernel(x, p, g_mix, w_in, q_norm, k_norm, g_ret, w_o, g_ffn, w_gate, w_up, w_down, g_ple,
           w_ple_gate, b_ple_gate, w_ple_proj):
    B, T, D = x.shape
    depth = w_in.shape[0]
    M = B * T
    assert w_in.shape[2] == 3 * ATTN_WIDTH + 4 * RET_WIDTH and ATTN_WIDTH == RET_WIDTH

    cos_tab, sin_tab = _rope_tables(T)
    log_decay = jnp.log(1.0 - jnp.power(2.0, -5.0 - jnp.arange(RET_HEADS, dtype=F32)))
    ones = jnp.ones((HEAD_DIM,), F32)

    xf = x.reshape(M, D)
    for i in range(depth):
        gains = jnp.stack([q_norm[i] * (LOG2_E * HEAD_DIM ** -0.5), k_norm[i], ones, ones,
                           ones * (HEAD_DIM ** -0.5), ones, ones])
        z = _in_projection(xf, g_mix[i][None], w_in[i].astype(BF16), gains[:, None, :],
                           cos_tab, sin_tab, T)
        a_out = _moba_attention(z, B, T)
        r_out = _retention(z, log_decay, g_ret[i][None], B, T)
        x1, h2 = _out_projection(a_out, r_out, w_o[i].astype(BF16), xf, g_ffn[i][None])
        y = _ffn(h2, w_gate[i].astype(BF16), w_up[i].astype(BF16), w_down[i].astype(BF16))
        xf = _ple(x1, y, p[i].reshape(M, -1), g_ple[i][None], w_ple_gate[i].astype(BF16),
                  b_ple_gate[i][None], w_ple_proj[i].astype(BF16))
    return xf.reshape(B, T, D)
```

```python
import functools

import jax
import jax.numpy as jnp
from jax import lax
from jax.experimental import pallas as pl
from jax.experimental.pallas import tpu as pltpu

HEAD_DIM = 128
ATTN_HEADS = 8
RET_HEADS = 8
ATTN_WIDTH = ATTN_HEADS * HEAD_DIM
RET_WIDTH = RET_HEADS * HEAD_DIM
MOBA_BLOCK = 256
MOBA_TOPK = 3
ROPE_THETA = 10000.0
RMS_EPS = 1e-6
NEG_INF = -1e30
LOG2_E = 1.4426950408889634

N_SEGMENTS = 7
SEG_AQ, SEG_AK, SEG_AV, SEG_RQ, SEG_RK, SEG_RV, SEG_RG = range(N_SEGMENTS)
Z_AQ, Z_AK, Z_RQ, Z_RK, Z_RV, Z_RG = range(N_SEGMENTS - 1)
ROPE_ATTN, ROPE_RET = 0, 1

LANES = 128
VMEM_LIMIT_BYTES = 56 * 1024 * 1024

F32 = jnp.float32
BF16 = jnp.bfloat16

_NT = (((1,), (1,)), ((), ()))
_TN = (((0,), (0,)), ((), ()))


def _sigmoid(x):
    return 1.0 / (1.0 + jnp.exp(-x))


def _rms_scale(x):
    return lax.rsqrt(jnp.mean(x * x, axis=-1, keepdims=True) + RMS_EPS)


def _pick(dim, pref):
    t = min(dim, pref)
    while dim % t:
        t //= 2
    return t


def _inproj_kernel(x_ref, gmix_ref, w_ref, gain_ref, cos_ref, sin_ref, z_ref, vt_ref, h_scr, *,
                   blocks_per_seg, slab):
    j = pl.program_id(1)
    seg = j // blocks_per_seg

    @pl.when(j == 0)
    def _():
        x = x_ref[...]
        h_scr[...] = (x * _rms_scale(x) * gmix_ref[...]).astype(h_scr.dtype)

    def step(norm, rope, transposed=False):
        heads_per_slab = slab // HEAD_DIM
        n_slabs = w_ref.shape[1] // slab
        ones_bf = jnp.ones((HEAD_DIM, HEAD_DIM), BF16)

        def slab_dot(s):
            return jnp.dot(h_scr[...], w_ref[:, s * slab:(s + 1) * slab], preferred_element_type=F32)

        acc_next = slab_dot(0)
        for s in range(n_slabs):
            acc = acc_next
            if s + 1 < n_slabs:
                acc_next = slab_dot(s + 1)
            for hh in range(heads_per_slab):
                y = acc[:, hh * HEAD_DIM:(hh + 1) * HEAD_DIM]
                out = y
                if rope:
                    y = y * gain_ref[0]
                    out = y * cos_ref[0] + pltpu.roll(y, HEAD_DIM // 2, 1) * sin_ref[0]
                if norm:
                    y0 = acc[:, hh * HEAD_DIM:(hh + 1) * HEAD_DIM]
                    sq = y0 * y0
                    hi = sq.astype(BF16)
                    lo = (sq - hi.astype(F32)).astype(BF16)
                    ssum = (jnp.dot(hi, ones_bf, preferred_element_type=F32)
                            + jnp.dot(lo, ones_bf, preferred_element_type=F32))
                    out = out * lax.rsqrt(ssum * (1.0 / HEAD_DIM) + RMS_EPS)
                if transposed:
                    vt_ref[s * heads_per_slab + hh] = out.T.astype(vt_ref.dtype)
                else:
                    z_ref[s * heads_per_slab + hh] = out.astype(z_ref.dtype)

    is_attn_qk = seg <= SEG_AK
    is_ret_qk = (seg == SEG_RQ) | (seg == SEG_RK)
    is_attn_v = seg == SEG_AV
    pl.when(is_attn_qk)(lambda: step(norm=True, rope=True))
    pl.when(is_ret_qk)(lambda: step(norm=False, rope=True))
    pl.when(is_attn_v)(lambda: step(norm=False, rope=False, transposed=True))
    pl.when(jnp.logical_not(is_attn_qk | is_ret_qk | is_attn_v))(lambda: step(norm=False, rope=False))


def _in_projection(x, g_mix, w_in, gains, cos_tab, sin_tab, seq_len):
    M, D = x.shape
    n_cols = w_in.shape[1]
    tm = _pick(seq_len, 1024)
    tn = _pick(ATTN_WIDTH, 1024)
    blocks_per_seg = ATTN_WIDTH // tn
    t_blocks = seq_len // tm

    def seg_of(j):
        return j // blocks_per_seg

    def rope_of(j):
        return jnp.where(seg_of(j) <= SEG_AV, ROPE_ATTN, ROPE_RET)

    av_first = SEG_AV * blocks_per_seg

    def z_block(j):
        return jnp.where(j < av_first, j, jnp.where(j < av_first + blocks_per_seg, av_first - 1, j - blocks_per_seg))

    def vt_block(j):
        return jnp.clip(j - av_first, 0, blocks_per_seg - 1)

    kern = functools.partial(_inproj_kernel, blocks_per_seg=blocks_per_seg, slab=_pick(tn, 256))
    heads_per_block = tn // HEAD_DIM
    return pl.pallas_call(
        kern,
        grid=(M // tm, n_cols // tn),
        in_specs=[
            pl.BlockSpec((tm, D), lambda i, j: (i, 0)),
            pl.BlockSpec((1, D), lambda i, j: (0, 0)),
            pl.BlockSpec((D, tn), lambda i, j: (0, j)),
            pl.BlockSpec((1, 1, HEAD_DIM), lambda i, j: (seg_of(j), 0, 0)),
            pl.BlockSpec((1, tm, HEAD_DIM), lambda i, j: (rope_of(j), i % t_blocks, 0)),
            pl.BlockSpec((1, tm, HEAD_DIM), lambda i, j: (rope_of(j), i % t_blocks, 0)),
        ],
        out_specs=[
            pl.BlockSpec((heads_per_block, tm, HEAD_DIM), lambda i, j: (z_block(j), i, 0)),
            pl.BlockSpec((heads_per_block, HEAD_DIM, tm), lambda i, j: (vt_block(j), 0, i)),
        ],
        out_shape=[
            jax.ShapeDtypeStruct(((N_SEGMENTS - 1) * ATTN_HEADS, M, HEAD_DIM), BF16),
            jax.ShapeDtypeStruct((ATTN_HEADS, HEAD_DIM, M), BF16),
        ],
        scratch_shapes=[pltpu.VMEM((tm, D), BF16)],
        compiler_params=pltpu.CompilerParams(
            dimension_semantics=("parallel", "arbitrary"), vmem_limit_bytes=VMEM_LIMIT_BYTES),
        name="in_projection",
    )(x, g_mix, w_in, gains, cos_tab, sin_tab)


def _moba_kernel(q_ref, k_ref, vt_ref, o_ref, kmean_scr, sa_scr, sb_scr, acc_scr, *,
                 n_blocks, group, heads, q_blocks):
    BS = MOBA_BLOCK
    G = group
    BQ = q_blocks * BS
    step = pl.program_id(1)
    first_own = step * q_blocks

    @pl.when(step == 0)
    def _():
        for hh in range(heads):
            for n in range(n_blocks):
                kb = k_ref[hh, n * BS:(n + 1) * BS, :].astype(F32)
                kmean_scr[hh, n:n + 1, :] = jnp.mean(kb, axis=0, keepdims=True)

    assert BS & (BS - 1) == 0
    q_blk = first_own + lax.shift_right_logical(lax.broadcasted_iota(jnp.int32, (1, BQ), 1),
                                                BS.bit_length() - 1)

    def select_blocks(qt, hh):
        gate = jnp.dot(kmean_scr[hh].astype(BF16), qt, preferred_element_type=F32)
        blk = lax.broadcasted_iota(jnp.int32, gate.shape, 0)
        blk_f = blk.astype(F32)
        gate = jnp.where(blk < q_blk, gate, NEG_INF)
        sel = []
        for r in range(MOBA_TOPK):
            mx = jnp.max(gate, axis=0, keepdims=True)
            pick = jnp.min(jnp.where(gate == mx, blk_f, float(n_blocks)), axis=0, keepdims=True)
            sel.append(jnp.where(r < q_blk, pick, -1.0))
            gate = jnp.where(blk_f == pick, -jnp.inf, gate)
        return sel

    def row_bias(sel, n):
        n_f = lax.convert_element_type(n, F32)
        allowed = (sel[0] == n_f) | (sel[1] == n_f) | (sel[2] == n_f)
        return jnp.where(allowed, 0.0, NEG_INF)

    def past_bias(sel, n):
        return jnp.where(n < first_own, row_bias(sel, n), NEG_INF)

    qts = [q_ref[hh].astype(F32).T.astype(BF16) for hh in range(heads)]
    all_heads = range(heads)

    def score_dots(start, size):
        return [jnp.dot(k_ref[hh, pl.ds(start, size), :], qts[hh], preferred_element_type=F32)
                for hh in all_heads]

    def value_dot(hh, start, size, pb):
        return jnp.dot(vt_ref[hh, :, pl.ds(start, size)], pb, preferred_element_type=F32)

    def group_start(g):
        return pl.multiple_of(g * (G * BS), G * BS)

    def score_finish(g, ss, buf, m_run):
        m_new, alpha = [], []
        for hh in all_heads:
            buf[hh] = ss[hh]
            m = m_run[hh]
            for t in range(G):
                tile_max = jnp.max(ss[hh][t * BS:(t + 1) * BS], axis=0, keepdims=True)
                m = jnp.maximum(m, tile_max + past_bias(sels[hh], g * G + t))
            m_new.append(m)
            alpha.append(jnp.exp2(m_run[hh] - m))
        return m_new, alpha

    def score_stage(g, buf, m_run):
        return score_finish(g, score_dots(group_start(g), G * BS), buf, m_run)

    def value_stage(g, buf, m_g, alpha, l):
        start = group_start(g)
        l_new, pbs = [], []
        for hh in all_heads:
            p = [jnp.exp2(buf[hh, t * BS:(t + 1) * BS, :] + (past_bias(sels[hh], g * G + t) - m_g[hh]))
                 for t in range(G)]
            l_new.append(functools.reduce(
                jnp.add, [jnp.sum(pt, axis=0, keepdims=True) for pt in p], alpha[hh] * l[hh]))
            pbs.append(jnp.concatenate([pt.astype(BF16) for pt in p], axis=0))
        for hh in all_heads:
            acc_scr[hh] = alpha[hh] * acc_scr[hh] + value_dot(hh, start, G * BS, pbs[hh])
        return l_new

    n_pairs = jnp.maximum((first_own + 2 * G - 1) // (2 * G), 1)

    sels = [select_blocks(qts[hh], hh) for hh in all_heads]

    own_start = pl.multiple_of(first_own * BS, BQ)
    ss_own = score_dots(own_start, BQ)
    ss_0 = score_dots(group_start(0), G * BS)
    kpos = lax.broadcasted_iota(jnp.int32, (BS, BQ), 0)
    qpos = lax.broadcasted_iota(jnp.int32, (BS, BQ), 1)
    own_base = [jnp.where((qpos >= t * BS) & (qpos < (t + 1) * BS) & (kpos <= qpos - t * BS), 0.0, NEG_INF)
                for t in range(q_blocks)]
    own_tiles = []
    for hh in all_heads:
        tiles = []
        for t in range(q_blocks):
            bias = own_base[t]
            if t + 1 < q_blocks:
                later = jnp.broadcast_to(row_bias(sels[hh], first_own + t), (BS, BQ))
                bias = jnp.where(qpos >= (t + 1) * BS, later, bias)
            tiles.append(ss_own[hh][t * BS:(t + 1) * BS] + bias)
        own_tiles.append(tiles)
    ms = [functools.reduce(jnp.maximum, [jnp.max(s, axis=0, keepdims=True) for s in tiles])
          for tiles in own_tiles]
    ps = [[jnp.exp2(s - ms[hh]) for s in own_tiles[hh]] for hh in all_heads]
    l = [functools.reduce(jnp.add, [jnp.sum(p, axis=0, keepdims=True) for p in ps[hh]]) for hh in all_heads]
    m_a, alpha_a = score_finish(0, ss_0, sa_scr, ms)
    for hh in all_heads:
        acc_scr[hh] = value_dot(hh, own_start, BQ, jnp.concatenate([p.astype(BF16) for p in ps[hh]], axis=0))

    def pair(k, carry):
        m_a, alpha_a, l = carry
        m_b, alpha_b = score_stage(2 * k + 1, sb_scr, m_a)
        l = value_stage(2 * k, sa_scr, m_a, alpha_a, l)
        m_n, alpha_n = score_stage(2 * k + 2, sa_scr, m_b)
        l = value_stage(2 * k + 1, sb_scr, m_b, alpha_b, l)
        return m_n, alpha_n, l

    m_a, alpha_a, l = lax.fori_loop(0, n_pairs - 1, pair, (m_a, alpha_a, l))
    last = 2 * (n_pairs - 1)
    m_b, alpha_b = score_stage(last + 1, sb_scr, m_a)
    l = value_stage(last, sa_scr, m_a, alpha_a, l)
    l = value_stage(last + 1, sb_scr, m_b, alpha_b, l)
    for hh in all_heads:
        o_ref[:, hh * HEAD_DIM:(hh + 1) * HEAD_DIM] = (acc_scr[hh] * (1.0 / l[hh])).T.astype(o_ref.dtype)


def _moba_attention(z, vt, batch, seq_len):
    M = z.shape[1]
    BS = MOBA_BLOCK
    n_blocks = seq_len // BS
    H = ATTN_HEADS
    heads = 4
    group = 2
    q_blocks = 2
    assert seq_len % BS == 0 and n_blocks - 1 >= MOBA_TOPK
    assert n_blocks % (2 * group) == 0 and n_blocks % q_blocks == 0
    hb = H // heads
    steps = n_blocks // q_blocks
    BQ = q_blocks * BS
    kern = functools.partial(_moba_kernel, n_blocks=n_blocks, group=group, heads=heads, q_blocks=q_blocks)
    return pl.pallas_call(
        kern,
        grid=(batch * hb, steps),
        in_specs=[
            pl.BlockSpec((heads, BQ, HEAD_DIM), lambda bh, s: (Z_AQ * hb + bh % hb, (bh // hb) * steps + s, 0)),
            pl.BlockSpec((heads, seq_len, HEAD_DIM), lambda bh, s: (Z_AK * hb + bh % hb, bh // hb, 0)),
            pl.BlockSpec((heads, HEAD_DIM, seq_len), lambda bh, s: (bh % hb, 0, bh // hb)),
        ],
        out_specs=pl.BlockSpec((BQ, heads * HEAD_DIM), lambda bh, s: ((bh // hb) * steps + s, bh % hb)),
        out_shape=jax.ShapeDtypeStruct((M, ATTN_WIDTH), BF16),
        scratch_shapes=[pltpu.VMEM((heads, n_blocks, HEAD_DIM), F32),
                        pltpu.VMEM((heads, group * BS, BQ), F32),
                        pltpu.VMEM((heads, group * BS, BQ), F32),
                        pltpu.VMEM((heads, HEAD_DIM, BQ), F32)],
        compiler_params=pltpu.CompilerParams(
            dimension_semantics=("parallel", "arbitrary"), vmem_limit_bytes=VMEM_LIMIT_BYTES),
        name="moba_attention",
    )(z, z, vt)


def _retention_kernel(lg_ref, q_ref, k_ref, v_ref, g_ref, gret_ref, o_ref, s_scr, *, chunk, n_chunks):
    C = chunk
    head = pl.program_id(0) % RET_HEADS
    lg = lg_ref[head]

    @pl.when(pl.program_id(1) == 0)
    def _():
        s_scr[...] = jnp.zeros_like(s_scr)

    row = lax.broadcasted_iota(jnp.int32, (C, C), 0)
    col = lax.broadcasted_iota(jnp.int32, (C, C), 1)
    rel = (row - col).astype(F32)
    decay = jnp.where(rel >= 0, jnp.exp(lg * jnp.maximum(rel, 0.0)), 0.0)
    pos = lax.broadcasted_iota(jnp.int32, (C, HEAD_DIM), 0).astype(F32)
    k_decay = jnp.exp(lg * (C - 1.0 - pos))
    q_decay = jnp.exp(lg * (pos + 1.0))
    chunk_decay = jnp.exp(jnp.full((1, HEAD_DIM), lg * C, F32))
    gret = gret_ref[...]

    S = s_scr[...]
    for c in range(n_chunks):
        sl = slice(c * C, (c + 1) * C)
        q = q_ref[0, sl, :]
        k = k_ref[0, sl, :]
        v = v_ref[0, sl, :]
        s = lax.dot_general(q, k, _NT, preferred_element_type=F32) * decay
        intra = jnp.dot(s.astype(BF16), v, preferred_element_type=F32)
        cross = jnp.dot((q.astype(F32) * q_decay).astype(BF16), S.astype(BF16),
                        preferred_element_type=F32)
        r = intra + cross
        U = lax.dot_general((k.astype(F32) * k_decay).astype(BF16), v, _TN, preferred_element_type=F32)
        S = chunk_decay * S + U
        r = r * _rms_scale(r) * gret
        gt = g_ref[0, sl, :].astype(F32)
        o_ref[sl, :] = (gt * _sigmoid(gt) * r).astype(o_ref.dtype)
    s_scr[...] = S


def _retention(z, log_decay, g_ret, batch, seq_len):
    M = z.shape[1]
    H = RET_HEADS
    chunk = _pick(seq_len, 256)
    tb = _pick(seq_len, 4096)
    t_blocks = seq_len // tb

    def zspec(seg):
        return pl.BlockSpec((1, tb, HEAD_DIM), lambda bh, t: (seg * H + bh % H, (bh // H) * t_blocks + t, 0))

    kern = functools.partial(_retention_kernel, chunk=chunk, n_chunks=tb // chunk)
    return pl.pallas_call(
        kern,
        grid=(batch * H, t_blocks),
        in_specs=[
            pl.BlockSpec(memory_space=pltpu.SMEM),
            zspec(Z_RQ), zspec(Z_RK), zspec(Z_RV), zspec(Z_RG),
            pl.BlockSpec((1, HEAD_DIM), lambda bh, t: (0, bh % H)),
        ],
        out_specs=pl.BlockSpec((tb, HEAD_DIM), lambda bh, t: ((bh // H) * t_blocks + t, bh % H)),
        out_shape=jax.ShapeDtypeStruct((M, RET_WIDTH), BF16),
        scratch_shapes=[pltpu.VMEM((HEAD_DIM, HEAD_DIM), F32)],
        compiler_params=pltpu.CompilerParams(
            dimension_semantics=("parallel", "arbitrary"), vmem_limit_bytes=VMEM_LIMIT_BYTES),
        name="retention",
    )(log_decay, z, z, z, z, g_ret)


def _oproj_kernel(a_ref, r_ref, wa_ref, wr_ref, x_ref, g_ref, x1_ref, h_ref):
    mix = (jnp.dot(a_ref[...], wa_ref[...], preferred_element_type=F32)
           + jnp.dot(r_ref[...], wr_ref[...], preferred_element_type=F32))
    x1 = x_ref[...] + mix
    x1_ref[...] = x1
    h_ref[...] = (x1 * _rms_scale(x1) * g_ref[...]).astype(h_ref.dtype)


def _out_projection(a_out, r_out, w_o, x, g_ffn):
    M, D = x.shape
    tm = _pick(M, 512)
    assert ATTN_WIDTH == RET_WIDTH
    return pl.pallas_call(
        _oproj_kernel,
        grid=(M // tm,),
        in_specs=[
            pl.BlockSpec((tm, ATTN_WIDTH), lambda i: (i, 0)),
            pl.BlockSpec((tm, RET_WIDTH), lambda i: (i, 0)),
            pl.BlockSpec((ATTN_WIDTH, D), lambda i: (0, 0)),
            pl.BlockSpec((RET_WIDTH, D), lambda i: (1, 0)),
            pl.BlockSpec((tm, D), lambda i: (i, 0)),
            pl.BlockSpec((1, D), lambda i: (0, 0)),
        ],
        out_specs=[pl.BlockSpec((tm, D), lambda i: (i, 0)), pl.BlockSpec((tm, D), lambda i: (i, 0))],
        out_shape=[jax.ShapeDtypeStruct((M, D), F32), jax.ShapeDtypeStruct((M, D), BF16)],
        compiler_params=pltpu.CompilerParams(
            dimension_semantics=("parallel",), vmem_limit_bytes=VMEM_LIMIT_BYTES),
        name="out_projection",
    )(a_out, r_out, w_o, w_o, x, g_ffn)


def _ffn_kernel(h_ref, wg_ref, wu_ref, wd_ref, y_ref):
    @pl.when(pl.program_id(1) == 0)
    def _():
        y_ref[...] = jnp.zeros_like(y_ref)

    h = h_ref[...]
    gate = jnp.dot(h, wg_ref[...], preferred_element_type=F32)
    up = jnp.dot(h, wu_ref[...], preferred_element_type=F32)
    act = (gate * _sigmoid(gate) * up).astype(BF16)
    y_ref[...] += jnp.dot(act, wd_ref[...], preferred_element_type=F32)


def _ffn(h2, w_gate, w_up, w_down):
    M, D = h2.shape
    F = w_gate.shape[1]
    tm = _pick(M, 1024)
    tf = _pick(F, 512)
    return pl.pallas_call(
        _ffn_kernel,
        grid=(M // tm, F // tf),
        in_specs=[
            pl.BlockSpec((tm, D), lambda i, f: (i, 0)),
            pl.BlockSpec((D, tf), lambda i, f: (0, f)),
            pl.BlockSpec((D, tf), lambda i, f: (0, f)),
            pl.BlockSpec((tf, D), lambda i, f: (f, 0)),
        ],
        out_specs=pl.BlockSpec((tm, D), lambda i, f: (i, 0)),
        out_shape=jax.ShapeDtypeStruct((M, D), F32),
        compiler_params=pltpu.CompilerParams(
            dimension_semantics=("parallel", "arbitrary"), vmem_limit_bytes=VMEM_LIMIT_BYTES),
        name="swiglu_ffn",
    )(h2, w_gate, w_up, w_down)


def _ple_kernel(x1_ref, y_ref, p_ref, g_ref, wg_ref, b_ref, wp_ref, o_ref):
    x2 = x1_ref[...] + y_ref[...]
    h = (x2 * _rms_scale(x2) * g_ref[...]).astype(BF16)
    gate = _sigmoid(jnp.dot(h, wg_ref[...], preferred_element_type=F32) + b_ref[...])
    proj = jnp.dot(p_ref[...].astype(BF16), wp_ref[...], preferred_element_type=F32)
    o_ref[...] = x2 + proj * gate


def _ple(x1, y, p, g_ple, w_gate, b_gate, w_proj):
    M, D = x1.shape
    P = p.shape[1]
    tm = _pick(M, 512)
    return pl.pallas_call(
        _ple_kernel,
        grid=(M // tm,),
        in_specs=[
            pl.BlockSpec((tm, D), lambda i: (i, 0)),
            pl.BlockSpec((tm, D), lambda i: (i, 0)),
            pl.BlockSpec((tm, P), lambda i: (i, 0)),
            pl.BlockSpec((1, D), lambda i: (0, 0)),
            pl.BlockSpec((D, D), lambda i: (0, 0)),
            pl.BlockSpec((1, D), lambda i: (0, 0)),
            pl.BlockSpec((P, D), lambda i: (0, 0)),
        ],
        out_specs=pl.BlockSpec((tm, D), lambda i: (i, 0)),
        out_shape=jax.ShapeDtypeStruct((M, D), F32),
        compiler_params=pltpu.CompilerParams(
            dimension_semantics=("parallel",), vmem_limit_bytes=VMEM_LIMIT_BYTES),
        name="gated_ple",
    )(x1, y, p, g_ple, w_gate, b_gate, w_proj)


def _rope_tables(seq_len):
    pos = jnp.arange(seq_len, dtype=F32)
    inv_attn = jnp.power(ROPE_THETA, -jnp.arange(0, HEAD_DIM, 2, dtype=F32) / HEAD_DIM)
    inv_ret = jnp.power(ROPE_THETA, -jnp.linspace(0.0, 1.0, HEAD_DIM // 2, dtype=F32))

    def table(inv_freq):
        ang = pos[:, None] * inv_freq[None, :]
        c, s = jnp.cos(ang), jnp.sin(ang)
        return jnp.concatenate([c, c], axis=-1), jnp.concatenate([-s, s], axis=-1)

    ca, sa = table(inv_attn)
    cr, sr = table(inv_ret)
    return jnp.stack([ca, cr]), jnp.stack([sa, sr])


def kernel(x, p, g_mix, w_in, q_norm, k_norm, g_ret, w_o, g_ffn, w_gate, w_up, w_down, g_ple,
           w_ple_gate, b_ple_gate, w_ple_proj):
    B, T, D = x.shape
    depth = w_in.shape[0]
    M = B * T
    assert w_in.shape[2] == 3 * ATTN_WIDTH + 4 * RET_WIDTH and ATTN_WIDTH == RET_WIDTH

    cos_tab, sin_tab = _rope_tables(T)
    log_decay = jnp.log(1.0 - jnp.power(2.0, -5.0 - jnp.arange(RET_HEADS, dtype=F32)))
    ones = jnp.ones((HEAD_DIM,), F32)

    xf = x.reshape(M, D)
    for i in range(depth):
        gains = jnp.stack([q_norm[i] * (LOG2_E * HEAD_DIM ** -0.5), k_norm[i], ones, ones,
                           ones * (HEAD_DIM ** -0.5), ones, ones])
        z, vt = _in_projection(xf, g_mix[i][None], w_in[i].astype(BF16), gains[:, None, :],
                               cos_tab, sin_tab, T)
        a_out = _moba_attention(z, vt, B, T)
        r_out = _retention(z, log_decay, g_ret[i][None], B, T)
        x1, h2 = _out_projection(a_out, r_out, w_o[i].astype(BF16), xf, g_ffn[i][None])
        y = _ffn(h2, w_gate[i].astype(BF16), w_up[i].astype(BF16), w_down[i].astype(BF16))
        xf = _ple(x1, y, p[i].reshape(M, -1), g_ple[i][None], w_ple_gate[i].astype(BF16),
                  b_ple_gate[i][None], w_ple_proj[i].astype(BF16))
    return xf.reshape(B, T, D)
```

```python
import functools

import jax
import jax.numpy as jnp
from jax import lax
from jax.experimental import pallas as pl
from jax.experimental.pallas import tpu as pltpu

HEAD_DIM = 128
ATTN_HEADS = 8
RET_HEADS = 8
ATTN_WIDTH = ATTN_HEADS * HEAD_DIM
RET_WIDTH = RET_HEADS * HEAD_DIM
MOBA_BLOCK = 256
MOBA_TOPK = 3
ROPE_THETA = 10000.0
RMS_EPS = 1e-6
NEG_INF = -1e30
LOG2_E = 1.4426950408889634

N_SEGMENTS = 7
SEG_AQ, SEG_AK, SEG_AV, SEG_RQ, SEG_RK, SEG_RV, SEG_RG = range(N_SEGMENTS)
Z_AQ, Z_AK, Z_RQ, Z_RK, Z_RV, Z_RG = range(N_SEGMENTS - 1)
ROPE_ATTN, ROPE_RET = 0, 1

LANES = 128
VMEM_LIMIT_BYTES = 56 * 1024 * 1024

F32 = jnp.float32
BF16 = jnp.bfloat16

_NT = (((1,), (1,)), ((), ()))
_TN = (((0,), (0,)), ((), ()))


def _sigmoid(x):
    return 1.0 / (1.0 + jnp.exp(-x))


def _rms_scale(x):
    return lax.rsqrt(jnp.mean(x * x, axis=-1, keepdims=True) + RMS_EPS)


def _pick(dim, pref):
    t = min(dim, pref)
    while dim % t:
        t //= 2
    return t


def _inproj_kernel(x_ref, gmix_ref, w_ref, gain_ref, cos_ref, sin_ref, z_ref, vt_ref, h_scr, *,
                   blocks_per_seg, slab):
    j = pl.program_id(1)
    seg = j // blocks_per_seg

    @pl.when(j == 0)
    def _():
        x = x_ref[...]
        h_scr[...] = (x * _rms_scale(x) * gmix_ref[...]).astype(h_scr.dtype)

    def step(norm, rope, transposed=False):
        heads_per_slab = slab // HEAD_DIM
        n_slabs = w_ref.shape[1] // slab
        ones_bf = jnp.ones((HEAD_DIM, HEAD_DIM), BF16)

        def slab_dot(s):
            return jnp.dot(h_scr[...], w_ref[:, s * slab:(s + 1) * slab], preferred_element_type=F32)

        acc_next = slab_dot(0)
        for s in range(n_slabs):
            acc = acc_next
            if s + 1 < n_slabs:
                acc_next = slab_dot(s + 1)
            for hh in range(heads_per_slab):
                y = acc[:, hh * HEAD_DIM:(hh + 1) * HEAD_DIM]
                out = y
                if rope:
                    y = y * gain_ref[0]
                    out = y * cos_ref[0] + pltpu.roll(y, HEAD_DIM // 2, 1) * sin_ref[0]
                if norm:
                    y0 = acc[:, hh * HEAD_DIM:(hh + 1) * HEAD_DIM]
                    sq = y0 * y0
                    hi = sq.astype(BF16)
                    lo = (sq - hi.astype(F32)).astype(BF16)
                    ssum = (jnp.dot(hi, ones_bf, preferred_element_type=F32)
                            + jnp.dot(lo, ones_bf, preferred_element_type=F32))
                    out = out * lax.rsqrt(ssum * (1.0 / HEAD_DIM) + RMS_EPS)
                if transposed:
                    vt_ref[s * heads_per_slab + hh] = out.T.astype(vt_ref.dtype)
                else:
                    z_ref[s * heads_per_slab + hh] = out.astype(z_ref.dtype)

    is_attn_qk = seg <= SEG_AK
    is_ret_qk = (seg == SEG_RQ) | (seg == SEG_RK)
    is_attn_v = seg == SEG_AV
    pl.when(is_attn_qk)(lambda: step(norm=True, rope=True))
    pl.when(is_ret_qk)(lambda: step(norm=False, rope=True))
    pl.when(is_attn_v)(lambda: step(norm=False, rope=False, transposed=True))
    pl.when(jnp.logical_not(is_attn_qk | is_ret_qk | is_attn_v))(lambda: step(norm=False, rope=False))


def _in_projection(x, g_mix, w_in, gains, cos_tab, sin_tab, seq_len):
    M, D = x.shape
    n_cols = w_in.shape[1]
    tm = _pick(seq_len, 1024)
    tn = _pick(ATTN_WIDTH, 1024)
    blocks_per_seg = ATTN_WIDTH // tn
    t_blocks = seq_len // tm

    def seg_of(j):
        return j // blocks_per_seg

    def rope_of(j):
        return jnp.where(seg_of(j) <= SEG_AV, ROPE_ATTN, ROPE_RET)

    av_first = SEG_AV * blocks_per_seg

    def z_block(j):
        return jnp.where(j < av_first, j, jnp.where(j < av_first + blocks_per_seg, av_first - 1, j - blocks_per_seg))

    def vt_block(j):
        return jnp.clip(j - av_first, 0, blocks_per_seg - 1)

    kern = functools.partial(_inproj_kernel, blocks_per_seg=blocks_per_seg, slab=_pick(tn, 256))
    heads_per_block = tn // HEAD_DIM
    return pl.pallas_call(
        kern,
        grid=(M // tm, n_cols // tn),
        in_specs=[
            pl.BlockSpec((tm, D), lambda i, j: (i, 0)),
            pl.BlockSpec((1, D), lambda i, j: (0, 0)),
            pl.BlockSpec((D, tn), lambda i, j: (0, j)),
            pl.BlockSpec((1, 1, HEAD_DIM), lambda i, j: (seg_of(j), 0, 0)),
            pl.BlockSpec((1, tm, HEAD_DIM), lambda i, j: (rope_of(j), i % t_blocks, 0)),
            pl.BlockSpec((1, tm, HEAD_DIM), lambda i, j: (rope_of(j), i % t_blocks, 0)),
        ],
        out_specs=[
            pl.BlockSpec((heads_per_block, tm, HEAD_DIM), lambda i, j: (z_block(j), i, 0)),
            pl.BlockSpec((heads_per_block, HEAD_DIM, tm), lambda i, j: (vt_block(j), 0, i)),
        ],
        out_shape=[
            jax.ShapeDtypeStruct(((N_SEGMENTS - 1) * ATTN_HEADS, M, HEAD_DIM), BF16),
            jax.ShapeDtypeStruct((ATTN_HEADS, HEAD_DIM, M), BF16),
        ],
        scratch_shapes=[pltpu.VMEM((tm, D), BF16)],
        compiler_params=pltpu.CompilerParams(
            dimension_semantics=("parallel", "arbitrary"), vmem_limit_bytes=VMEM_LIMIT_BYTES),
        name="in_projection",
    )(x, g_mix, w_in, gains, cos_tab, sin_tab)


def _moba_kernel(q_ref, k_ref, vt_ref, o_ref, kmean_scr, sa_scr, sb_scr, acc_scr, *,
                 n_blocks, group, heads, q_blocks):
    BS = MOBA_BLOCK
    G = group
    BQ = q_blocks * BS
    step = pl.program_id(1)
    first_own = step * q_blocks

    @pl.when(step == 0)
    def _():
        for hh in range(heads):
            for n in range(n_blocks):
                kb = k_ref[hh, n * BS:(n + 1) * BS, :].astype(F32)
                kmean_scr[hh, n:n + 1, :] = jnp.mean(kb, axis=0, keepdims=True)

    assert BS & (BS - 1) == 0
    q_blk = first_own + lax.shift_right_logical(lax.broadcasted_iota(jnp.int32, (1, BQ), 1),
                                                BS.bit_length() - 1)

    def select_blocks(qt, hh):
        gate = jnp.dot(kmean_scr[hh].astype(BF16), qt, preferred_element_type=F32)
        blk = lax.broadcasted_iota(jnp.int32, gate.shape, 0)
        blk_f = blk.astype(F32)
        gate = jnp.where(blk < q_blk, gate, NEG_INF)
        sel = []
        for r in range(MOBA_TOPK):
            mx = jnp.max(gate, axis=0, keepdims=True)
            pick = jnp.min(jnp.where(gate == mx, blk_f, float(n_blocks)), axis=0, keepdims=True)
            sel.append(jnp.where(r < q_blk, pick, -1.0))
            gate = jnp.where(blk_f == pick, -jnp.inf, gate)
        return sel

    def row_bias(sel, n):
        n_f = lax.convert_element_type(n, F32)
        allowed = (sel[0] == n_f) | (sel[1] == n_f) | (sel[2] == n_f)
        return jnp.where(allowed, 0.0, NEG_INF)

    def past_bias(sel, n):
        return jnp.where(n < first_own, row_bias(sel, n), NEG_INF)

    qts = [q_ref[hh].astype(F32).T.astype(BF16) for hh in range(heads)]
    all_heads = range(heads)

    def score_dots(start, size):
        return [jnp.dot(k_ref[hh, pl.ds(start, size), :], qts[hh], preferred_element_type=F32)
                for hh in all_heads]

    def value_dot(hh, start, size, pb):
        return jnp.dot(vt_ref[hh, :, pl.ds(start, size)], pb, preferred_element_type=F32)

    def group_start(g):
        return pl.multiple_of(g * (G * BS), G * BS)

    def score_finish(g, ss, buf, m_run):
        m_new, alpha = [], []
        for hh in all_heads:
            buf[hh] = ss[hh]
            m = m_run[hh]
            for t in range(G):
                tile_max = jnp.max(ss[hh][t * BS:(t + 1) * BS], axis=0, keepdims=True)
                m = jnp.maximum(m, tile_max + past_bias(sels[hh], g * G + t))
            m_new.append(m)
            alpha.append(jnp.exp2(m_run[hh] - m))
        return m_new, alpha

    def score_stage(g, buf, m_run):
        return score_finish(g, score_dots(group_start(g), G * BS), buf, m_run)

    def value_stage(g, buf, m_g, alpha, l):
        start = group_start(g)
        l_new, pbs = [], []
        for hh in all_heads:
            p = [jnp.exp2(buf[hh, t * BS:(t + 1) * BS, :] + (past_bias(sels[hh], g * G + t) - m_g[hh]))
                 for t in range(G)]
            l_new.append(functools.reduce(
                jnp.add, [jnp.sum(pt, axis=0, keepdims=True) for pt in p], alpha[hh] * l[hh]))
            pbs.append(jnp.concatenate([pt.astype(BF16) for pt in p], axis=0))
        for hh in all_heads:
            acc_scr[hh] = alpha[hh] * acc_scr[hh] + value_dot(hh, start, G * BS, pbs[hh])
        return l_new

    n_pairs = jnp.maximum((first_own + 2 * G - 1) // (2 * G), 1)

    sels = [select_blocks(qts[hh], hh) for hh in all_heads]

    own_start = pl.multiple_of(first_own * BS, BQ)
    ss_own = score_dots(own_start, BQ)
    ss_0 = score_dots(group_start(0), G * BS)
    kpos = lax.broadcasted_iota(jnp.int32, (BS, BQ), 0)
    qpos = lax.broadcasted_iota(jnp.int32, (BS, BQ), 1)
    own_base = [jnp.where((qpos >= t * BS) & (qpos < (t + 1) * BS) & (kpos <= qpos - t * BS), 0.0, NEG_INF)
                for t in range(q_blocks)]
    own_tiles = []
    for hh in all_heads:
        tiles = []
        for t in range(q_blocks):
            bias = own_base[t]
            if t + 1 < q_blocks:
                later = jnp.broadcast_to(row_bias(sels[hh], first_own + t), (BS, BQ))
                bias = jnp.where(qpos >= (t + 1) * BS, later, bias)
            tiles.append(ss_own[hh][t * BS:(t + 1) * BS] + bias)
        own_tiles.append(tiles)
    ms = [functools.reduce(jnp.maximum, [jnp.max(s, axis=0, keepdims=True) for s in tiles])
          for tiles in own_tiles]
    ps = [[jnp.exp2(s - ms[hh]) for s in own_tiles[hh]] for hh in all_heads]
    l = [functools.reduce(jnp.add, [jnp.sum(p, axis=0, keepdims=True) for p in ps[hh]]) for hh in all_heads]
    m_a, alpha_a = score_finish(0, ss_0, sa_scr, ms)
    for hh in all_heads:
        acc_scr[hh] = value_dot(hh, own_start, BQ, jnp.concatenate([p.astype(BF16) for p in ps[hh]], axis=0))

    def pair(k, carry):
        m_a, alpha_a, l = carry
        m_b, alpha_b = score_stage(2 * k + 1, sb_scr, m_a)
        l = value_stage(2 * k, sa_scr, m_a, alpha_a, l)
        m_n, alpha_n = score_stage(2 * k + 2, sa_scr, m_b)
        l = value_stage(2 * k + 1, sb_scr, m_b, alpha_b, l)
        return m_n, alpha_n, l

    m_a, alpha_a, l = lax.fori_loop(0, n_pairs - 1, pair, (m_a, alpha_a, l))
    last = 2 * (n_pairs - 1)
    m_b, alpha_b = score_stage(last + 1, sb_scr, m_a)
    l = value_stage(last, sa_scr, m_a, alpha_a, l)
    l = value_stage(last + 1, sb_scr, m_b, alpha_b, l)
    for hh in all_heads:
        o_ref[:, hh * HEAD_DIM:(hh + 1) * HEAD_DIM] = (acc_scr[hh] * (1.0 / l[hh])).T.astype(o_ref.dtype)


def _moba_attention(z, vt, batch, seq_len):
    M = z.shape[1]
    BS = MOBA_BLOCK
    n_blocks = seq_len // BS
    H = ATTN_HEADS
    heads = 4
    group = 2
    q_blocks = 2
    assert seq_len % BS == 0 and n_blocks - 1 >= MOBA_TOPK
    assert n_blocks % (2 * group) == 0 and n_blocks % q_blocks == 0
    hb = H // heads
    steps = n_blocks // q_blocks
    BQ = q_blocks * BS
    kern = functools.partial(_moba_kernel, n_blocks=n_blocks, group=group, heads=heads, q_blocks=q_blocks)
    return pl.pallas_call(
        kern,
        grid=(batch * hb, steps),
        in_specs=[
            pl.BlockSpec((heads, BQ, HEAD_DIM), lambda bh, s: (Z_AQ * hb + bh % hb, (bh // hb) * steps + s, 0)),
            pl.BlockSpec((heads, seq_len, HEAD_DIM), lambda bh, s: (Z_AK * hb + bh % hb, bh // hb, 0)),
            pl.BlockSpec((heads, HEAD_DIM, seq_len), lambda bh, s: (bh % hb, 0, bh // hb)),
        ],
        out_specs=pl.BlockSpec((BQ, heads * HEAD_DIM), lambda bh, s: ((bh // hb) * steps + s, bh % hb)),
        out_shape=jax.ShapeDtypeStruct((M, ATTN_WIDTH), BF16),
        scratch_shapes=[pltpu.VMEM((heads, n_blocks, HEAD_DIM), F32),
                        pltpu.VMEM((heads, group * BS, BQ), F32),
                        pltpu.VMEM((heads, group * BS, BQ), F32),
                        pltpu.VMEM((heads, HEAD_DIM, BQ), F32)],
        compiler_params=pltpu.CompilerParams(
            dimension_semantics=("parallel", "arbitrary"), vmem_limit_bytes=VMEM_LIMIT_BYTES),
        name="moba_attention",
    )(z, z, vt)


def _retention_kernel(lg_ref, q_ref, k_ref, v_ref, g_ref, gret_ref, o_ref, s_scr, *, chunk, n_chunks):
    C = chunk
    head = pl.program_id(0) % RET_HEADS
    lg = lg_ref[head]

    @pl.when(pl.program_id(1) == 0)
    def _():
        s_scr[...] = jnp.zeros_like(s_scr)

    row = lax.broadcasted_iota(jnp.int32, (C, C), 0)
    col = lax.broadcasted_iota(jnp.int32, (C, C), 1)
    rel = (row - col).astype(F32)
    decay = jnp.where(rel >= 0, jnp.exp(lg * jnp.maximum(rel, 0.0)), 0.0)
    pos = lax.broadcasted_iota(jnp.int32, (C, HEAD_DIM), 0).astype(F32)
    k_decay = jnp.exp(lg * (C - 1.0 - pos))
    q_decay = jnp.exp(lg * (pos + 1.0))
    chunk_decay = jnp.exp(jnp.full((1, HEAD_DIM), lg * C, F32))
    gret = gret_ref[...]

    S = s_scr[...]
    for c in range(n_chunks):
        sl = slice(c * C, (c + 1) * C)
        q = q_ref[0, sl, :]
        k = k_ref[0, sl, :]
        v = v_ref[0, sl, :]
        s = lax.dot_general(q, k, _NT, preferred_element_type=F32) * decay
        intra = jnp.dot(s.astype(BF16), v, preferred_element_type=F32)
        cross = jnp.dot((q.astype(F32) * q_decay).astype(BF16), S.astype(BF16),
                        preferred_element_type=F32)
        r = intra + cross
        U = lax.dot_general((k.astype(F32) * k_decay).astype(BF16), v, _TN, preferred_element_type=F32)
        S = chunk_decay * S + U
        r = r * _rms_scale(r) * gret
        gt = g_ref[0, sl, :].astype(F32)
        o_ref[sl, :] = (gt * _sigmoid(gt) * r).astype(o_ref.dtype)
    s_scr[...] = S


def _retention(z, log_decay, g_ret, batch, seq_len):
    M = z.shape[1]
    H = RET_HEADS
    chunk = _pick(seq_len, 256)
    tb = _pick(seq_len, 4096)
    t_blocks = seq_len // tb

    def zspec(seg):
        return pl.BlockSpec((1, tb, HEAD_DIM), lambda bh, t: (seg * H + bh % H, (bh // H) * t_blocks + t, 0))

    kern = functools.partial(_retention_kernel, chunk=chunk, n_chunks=tb // chunk)
    return pl.pallas_call(
        kern,
        grid=(batch * H, t_blocks),
        in_specs=[
            pl.BlockSpec(memory_space=pltpu.SMEM),
            zspec(Z_RQ), zspec(Z_RK), zspec(Z_RV), zspec(Z_RG),
            pl.BlockSpec((1, HEAD_DIM), lambda bh, t: (0, bh % H)),
        ],
        out_specs=pl.BlockSpec((tb, HEAD_DIM), lambda bh, t: ((bh // H) * t_blocks + t, bh % H)),
        out_shape=jax.ShapeDtypeStruct((M, RET_WIDTH), BF16),
        scratch_shapes=[pltpu.VMEM((HEAD_DIM, HEAD_DIM), F32)],
        compiler_params=pltpu.CompilerParams(
            dimension_semantics=("parallel", "arbitrary"), vmem_limit_bytes=VMEM_LIMIT_BYTES),
        name="retention",
    )(log_decay, z, z, z, z, g_ret)


def _oproj_kernel(a_ref, r_ref, wa_ref, wr_ref, x_ref, g_ref, x1_ref, h_ref):
    mix = (jnp.dot(a_ref[...], wa_ref[...], preferred_element_type=F32)
           + jnp.dot(r_ref[...], wr_ref[...], preferred_element_type=F32))
    x1 = x_ref[...] + mix
    x1_ref[...] = x1
    h_ref[...] = (x1 * _rms_scale(x1) * g_ref[...]).astype(h_ref.dtype)


def _out_projection(a_out, r_out, w_o, x, g_ffn):
    M, D = x.shape
    tm = _pick(M, 512)
    assert ATTN_WIDTH == RET_WIDTH
    return pl.pallas_call(
        _oproj_kernel,
        grid=(M // tm,),
        in_specs=[
            pl.BlockSpec((tm, ATTN_WIDTH), lambda i: (i, 0)),
            pl.BlockSpec((tm, RET_WIDTH), lambda i: (i, 0)),
            pl.BlockSpec((ATTN_WIDTH, D), lambda i: (0, 0)),
            pl.BlockSpec((RET_WIDTH, D), lambda i: (1, 0)),
            pl.BlockSpec((tm, D), lambda i: (i, 0)),
            pl.BlockSpec((1, D), lambda i: (0, 0)),
        ],
        out_specs=[pl.BlockSpec((tm, D), lambda i: (i, 0)), pl.BlockSpec((tm, D), lambda i: (i, 0))],
        out_shape=[jax.ShapeDtypeStruct((M, D), F32), jax.ShapeDtypeStruct((M, D), BF16)],
        compiler_params=pltpu.CompilerParams(
            dimension_semantics=("parallel",), vmem_limit_bytes=VMEM_LIMIT_BYTES),
        name="out_projection",
    )(a_out, r_out, w_o, w_o, x, g_ffn)


def _ffn_kernel(h_ref, wg_ref, wu_ref, wd_ref, y_ref):
    def step(first):
        h = h_ref[...]
        gate = jnp.dot(h, wg_ref[...].astype(BF16), preferred_element_type=F32)
        up = jnp.dot(h, wu_ref[...].astype(BF16), preferred_element_type=F32)
        act = (gate * _sigmoid(gate) * up).astype(BF16)
        part = jnp.dot(act, wd_ref[...].astype(BF16), preferred_element_type=F32)
        if first:
            y_ref[...] = part
        else:
            y_ref[...] += part

    f = pl.program_id(1)
    pl.when(f == 0)(lambda: step(True))
    pl.when(f > 0)(lambda: step(False))


def _ffn(h2, w_gate, w_up, w_down):
    M, D = h2.shape
    F = w_gate.shape[1]
    tm = _pick(M, 1024)
    tf = _pick(F, 512)
    return pl.pallas_call(
        _ffn_kernel,
        grid=(M // tm, F // tf),
        in_specs=[
            pl.BlockSpec((tm, D), lambda i, f: (i, 0)),
            pl.BlockSpec((D, tf), lambda i, f: (0, f)),
            pl.BlockSpec((D, tf), lambda i, f: (0, f)),
            pl.BlockSpec((tf, D), lambda i, f: (f, 0)),
        ],
        out_specs=pl.BlockSpec((tm, D), lambda i, f: (i, 0)),
        out_shape=jax.ShapeDtypeStruct((M, D), F32),
        compiler_params=pltpu.CompilerParams(
            dimension_semantics=("parallel", "arbitrary"), vmem_limit_bytes=VMEM_LIMIT_BYTES),
        name="swiglu_ffn",
    )(h2, w_gate, w_up, w_down)


def _ple_kernel(x1_ref, y_ref, p_ref, g_ref, wg_ref, b_ref, wp_ref, o_ref):
    x2 = x1_ref[...] + y_ref[...]
    h = (x2 * _rms_scale(x2) * g_ref[...]).astype(BF16)
    gate = _sigmoid(jnp.dot(h, wg_ref[...], preferred_element_type=F32) + b_ref[...])
    proj = jnp.dot(p_ref[...].astype(BF16), wp_ref[...], preferred_element_type=F32)
    o_ref[...] = x2 + proj * gate


def _ple(x1, y, p, g_ple, w_gate, b_gate, w_proj):
    M, D = x1.shape
    P = p.shape[1]
    tm = _pick(M, 512)
    return pl.pallas_call(
        _ple_kernel,
        grid=(M // tm,),
        in_specs=[
            pl.BlockSpec((tm, D), lambda i: (i, 0)),
            pl.BlockSpec((tm, D), lambda i: (i, 0)),
            pl.BlockSpec((tm, P), lambda i: (i, 0)),
            pl.BlockSpec((1, D), lambda i: (0, 0)),
            pl.BlockSpec((D, D), lambda i: (0, 0)),
            pl.BlockSpec((1, D), lambda i: (0, 0)),
            pl.BlockSpec((P, D), lambda i: (0, 0)),
        ],
        out_specs=pl.BlockSpec((tm, D), lambda i: (i, 0)),
        out_shape=jax.ShapeDtypeStruct((M, D), F32),
        compiler_params=pltpu.CompilerParams(
            dimension_semantics=("parallel",), vmem_limit_bytes=VMEM_LIMIT_BYTES),
        name="gated_ple",
    )(x1, y, p, g_ple, w_gate, b_gate, w_proj)


def _rope_tables(seq_len):
    pos = jnp.arange(seq_len, dtype=F32)
    inv_attn = jnp.power(ROPE_THETA, -jnp.arange(0, HEAD_DIM, 2, dtype=F32) / HEAD_DIM)
    inv_ret = jnp.power(ROPE_THETA, -jnp.linspace(0.0, 1.0, HEAD_DIM // 2, dtype=F32))

    def table(inv_freq):
        ang = pos[:, None] * inv_freq[None, :]
        c, s = jnp.cos(ang), jnp.sin(ang)
        return jnp.concatenate([c, c], axis=-1), jnp.concatenate([-s, s], axis=-1)

    ca, sa = table(inv_attn)
    cr, sr = table(inv_ret)
    return jnp.stack([ca, cr]), jnp.stack([sa, sr])


def kernel(x, p, g_mix, w_in, q_norm, k_norm, g_ret, w_o, g_ffn, w_gate, w_up, w_down, g_ple,
           w_ple_gate, b_ple_gate, w_ple_proj):
    B, T, D = x.shape
    depth = w_in.shape[0]
    M = B * T
    assert w_in.shape[2] == 3 * ATTN_WIDTH + 4 * RET_WIDTH and ATTN_WIDTH == RET_WIDTH

    cos_tab, sin_tab = _rope_tables(T)
    log_decay = jnp.log(1.0 - jnp.power(2.0, -5.0 - jnp.arange(RET_HEADS, dtype=F32)))
    ones = jnp.ones((HEAD_DIM,), F32)

    xf = x.reshape(M, D)
    for i in range(depth):
        gains = jnp.stack([q_norm[i] * (LOG2_E * HEAD_DIM ** -0.5), k_norm[i], ones, ones,
                           ones * (HEAD_DIM ** -0.5), ones, ones])
        z, vt = _in_projection(xf, g_mix[i][None], w_in[i].astype(BF16), gains[:, None, :],
                               cos_tab, sin_tab, T)
        a_out = _moba_attention(z, vt, B, T)
        r_out = _retention(z, log_decay, g_ret[i][None], B, T)
        x1, h2 = _out_projection(a_out, r_out, w_o[i].astype(BF16), xf, g_ffn[i][None])
        y = _ffn(h2, w_gate[i], w_up[i], w_down[i])
        xf = _ple(x1, y, p[i].reshape(M, -1), g_ple[i][None], w_ple_gate[i].astype(BF16),
                  b_ple_gate[i][None], w_ple_proj[i].astype(BF16))
    return xf.reshape(B, T, D)
```

```python
import functools

import jax
import jax.numpy as jnp
import numpy as np
from jax import lax
from jax.experimental import pallas as pl
from jax.experimental.pallas import tpu as pltpu

HEAD_DIM = 128
ATTN_HEADS = 8
RET_HEADS = 8
ATTN_WIDTH = ATTN_HEADS * HEAD_DIM
RET_WIDTH = RET_HEADS * HEAD_DIM
MOBA_BLOCK = 256
MOBA_TOPK = 3
ROPE_THETA = 10000.0
RMS_EPS = 1e-6
NEG_INF = -1e30
LOG2_E = 1.4426950408889634

N_SEGMENTS = 7
SEG_AQ, SEG_AK, SEG_AV, SEG_RQ, SEG_RK, SEG_RV, SEG_RG = range(N_SEGMENTS)
Z_AQ, Z_AK, Z_RQ, Z_RK, Z_RV, Z_RG = range(N_SEGMENTS - 1)
ROPE_ATTN, ROPE_RET = 0, 1

VMEM_LIMIT_BYTES = 56 * 1024 * 1024

F32 = jnp.float32
BF16 = jnp.bfloat16

_NT = (((1,), (1,)), ((), ()))
_TN = (((0,), (0,)), ((), ()))


def _sigmoid(x):
    return 1.0 / (1.0 + jnp.exp(-x))


def _rms_scale(x):
    return lax.rsqrt(jnp.mean(x * x, axis=-1, keepdims=True) + RMS_EPS)


def _pick(dim, pref):
    t = min(dim, pref)
    while dim % t:
        t //= 2
    return t


def _inproj_kernel(x_ref, gmix_ref, w_ref, gain_ref, cos_ref, sin_ref, z_ref, vt_ref, h_scr, *,
                   blocks_per_seg, slab):
    j = pl.program_id(1)
    seg = j // blocks_per_seg

    @pl.when(j == 0)
    def _():
        x = x_ref[...]
        h_scr[...] = (x * _rms_scale(x) * gmix_ref[...]).astype(h_scr.dtype)

    def step(norm, rope, transposed=False):
        heads_per_slab = slab // HEAD_DIM
        n_slabs = w_ref.shape[1] // slab
        ones_bf = jnp.ones((HEAD_DIM, HEAD_DIM), BF16)

        def slab_dot(s):
            return jnp.dot(h_scr[...], w_ref[:, s * slab:(s + 1) * slab], preferred_element_type=F32)

        acc_next = slab_dot(0)
        for s in range(n_slabs):
            acc = acc_next
            if s + 1 < n_slabs:
                acc_next = slab_dot(s + 1)
            for hh in range(heads_per_slab):
                y = acc[:, hh * HEAD_DIM:(hh + 1) * HEAD_DIM]
                out = y
                if rope:
                    y = y * gain_ref[0]
                    out = y * cos_ref[0] + pltpu.roll(y, HEAD_DIM // 2, 1) * sin_ref[0]
                if norm:
                    y0 = acc[:, hh * HEAD_DIM:(hh + 1) * HEAD_DIM]
                    sq = y0 * y0
                    hi = sq.astype(BF16)
                    lo = (sq - hi.astype(F32)).astype(BF16)
                    ssum = (jnp.dot(hi, ones_bf, preferred_element_type=F32)
                            + jnp.dot(lo, ones_bf, preferred_element_type=F32))
                    out = out * lax.rsqrt(ssum * (1.0 / HEAD_DIM) + RMS_EPS)
                if transposed:
                    vt_ref[s * heads_per_slab + hh] = out.T.astype(vt_ref.dtype)
                else:
                    z_ref[s * heads_per_slab + hh] = out.astype(z_ref.dtype)

    is_attn_qk = seg <= SEG_AK
    is_ret_qk = (seg == SEG_RQ) | (seg == SEG_RK)
    is_attn_v = seg == SEG_AV
    pl.when(is_attn_qk)(lambda: step(norm=True, rope=True))
    pl.when(is_ret_qk)(lambda: step(norm=False, rope=True))
    pl.when(is_attn_v)(lambda: step(norm=False, rope=False, transposed=True))
    pl.when(jnp.logical_not(is_attn_qk | is_ret_qk | is_attn_v))(lambda: step(norm=False, rope=False))


def _in_projection(x, g_mix, w_in, gains, cos_tab, sin_tab, seq_len):
    M, D = x.shape
    n_cols = w_in.shape[1]
    tm = _pick(seq_len, 1024)
    tn = _pick(ATTN_WIDTH, 1024)
    blocks_per_seg = ATTN_WIDTH // tn
    t_blocks = seq_len // tm

    def seg_of(j):
        return j // blocks_per_seg

    def rope_of(j):
        return jnp.where(seg_of(j) <= SEG_AV, ROPE_ATTN, ROPE_RET)

    av_first = SEG_AV * blocks_per_seg

    def z_block(j):
        return jnp.where(j < av_first, j, jnp.where(j < av_first + blocks_per_seg, av_first - 1, j - blocks_per_seg))

    def vt_block(j):
        return jnp.clip(j - av_first, 0, blocks_per_seg - 1)

    kern = functools.partial(_inproj_kernel, blocks_per_seg=blocks_per_seg, slab=_pick(tn, 256))
    heads_per_block = tn // HEAD_DIM
    return pl.pallas_call(
        kern,
        grid=(M // tm, n_cols // tn),
        in_specs=[
            pl.BlockSpec((tm, D), lambda i, j: (i, 0)),
            pl.BlockSpec((1, D), lambda i, j: (0, 0)),
            pl.BlockSpec((D, tn), lambda i, j: (0, j)),
            pl.BlockSpec((1, 1, HEAD_DIM), lambda i, j: (seg_of(j), 0, 0)),
            pl.BlockSpec((1, tm, HEAD_DIM), lambda i, j: (rope_of(j), i % t_blocks, 0)),
            pl.BlockSpec((1, tm, HEAD_DIM), lambda i, j: (rope_of(j), i % t_blocks, 0)),
        ],
        out_specs=[
            pl.BlockSpec((heads_per_block, tm, HEAD_DIM), lambda i, j: (z_block(j), i, 0)),
            pl.BlockSpec((heads_per_block, HEAD_DIM, tm), lambda i, j: (vt_block(j), 0, i)),
        ],
        out_shape=[
            jax.ShapeDtypeStruct(((N_SEGMENTS - 1) * ATTN_HEADS, M, HEAD_DIM), BF16),
            jax.ShapeDtypeStruct((ATTN_HEADS, HEAD_DIM, M), BF16),
        ],
        scratch_shapes=[pltpu.VMEM((tm, D), BF16)],
        compiler_params=pltpu.CompilerParams(
            dimension_semantics=("parallel", "arbitrary"), vmem_limit_bytes=VMEM_LIMIT_BYTES),
        name="in_projection",
    )(x, g_mix, w_in, gains, cos_tab, sin_tab)


def _moba_kernel(q_ref, k_ref, vt_ref, o_ref, kmean_scr, sa_scr, sb_scr, acc_scr, *,
                 n_blocks, group, heads, q_blocks):
    BS = MOBA_BLOCK
    G = group
    BQ = q_blocks * BS
    step = pl.program_id(1)
    first_own = step * q_blocks

    @pl.when(step == 0)
    def _():
        for hh in range(heads):
            for n in range(n_blocks):
                kb = k_ref[hh, n * BS:(n + 1) * BS, :].astype(F32)
                kmean_scr[hh, n:n + 1, :] = jnp.mean(kb, axis=0, keepdims=True)

    assert BS & (BS - 1) == 0
    q_blk = first_own + lax.shift_right_logical(lax.broadcasted_iota(jnp.int32, (1, BQ), 1),
                                                BS.bit_length() - 1)

    def select_blocks(qt, hh):
        gate = jnp.dot(kmean_scr[hh].astype(BF16), qt, preferred_element_type=F32)
        blk = lax.broadcasted_iota(jnp.int32, gate.shape, 0)
        blk_f = blk.astype(F32)
        gate = jnp.where(blk < q_blk, gate, NEG_INF)
        sel = []
        for r in range(MOBA_TOPK):
            mx = jnp.max(gate, axis=0, keepdims=True)
            pick = jnp.min(jnp.where(gate == mx, blk_f, float(n_blocks)), axis=0, keepdims=True)
            sel.append(jnp.where(r < q_blk, pick, -1.0))
            gate = jnp.where(blk_f == pick, -jnp.inf, gate)
        return sel

    def row_bias(sel, n):
        n_f = lax.convert_element_type(n, F32)
        allowed = (sel[0] == n_f) | (sel[1] == n_f) | (sel[2] == n_f)
        return jnp.where(allowed, 0.0, NEG_INF)

    def past_bias(sel, n):
        return jnp.where(n < first_own, row_bias(sel, n), NEG_INF)

    qts = [q_ref[hh].astype(F32).T.astype(BF16) for hh in range(heads)]
    all_heads = range(heads)

    def score_dots(start, size):
        return [jnp.dot(k_ref[hh, pl.ds(start, size), :], qts[hh], preferred_element_type=F32)
                for hh in all_heads]

    def value_dot(hh, start, size, pb):
        return jnp.dot(vt_ref[hh, :, pl.ds(start, size)], pb, preferred_element_type=F32)

    def group_start(g):
        return pl.multiple_of(g * (G * BS), G * BS)

    def score_finish(g, ss, buf, m_run):
        m_new, alpha = [], []
        for hh in all_heads:
            buf[hh] = ss[hh]
            m = m_run[hh]
            for t in range(G):
                tile_max = jnp.max(ss[hh][t * BS:(t + 1) * BS], axis=0, keepdims=True)
                m = jnp.maximum(m, tile_max + past_bias(sels[hh], g * G + t))
            m_new.append(m)
            alpha.append(jnp.exp2(m_run[hh] - m))
        return m_new, alpha

    def score_stage(g, buf, m_run):
        return score_finish(g, score_dots(group_start(g), G * BS), buf, m_run)

    def value_stage(g, buf, m_g, alpha, l):
        start = group_start(g)
        l_new, pbs = [], []
        for hh in all_heads:
            p = [jnp.exp2(buf[hh, t * BS:(t + 1) * BS, :] + (past_bias(sels[hh], g * G + t) - m_g[hh]))
                 for t in range(G)]
            l_new.append(functools.reduce(
                jnp.add, [jnp.sum(pt, axis=0, keepdims=True) for pt in p], alpha[hh] * l[hh]))
            pbs.append(jnp.concatenate([pt.astype(BF16) for pt in p], axis=0))
        for hh in all_heads:
            acc_scr[hh] = alpha[hh] * acc_scr[hh] + value_dot(hh, start, G * BS, pbs[hh])
        return l_new

    n_pairs = jnp.maximum((first_own + 2 * G - 1) // (2 * G), 1)

    sels = [select_blocks(qts[hh], hh) for hh in all_heads]

    own_start = pl.multiple_of(first_own * BS, BQ)
    ss_own = score_dots(own_start, BQ)
    ss_0 = score_dots(group_start(0), G * BS)
    kpos = lax.broadcasted_iota(jnp.int32, (BS, BQ), 0)
    qpos = lax.broadcasted_iota(jnp.int32, (BS, BQ), 1)
    own_base = [jnp.where((qpos >= t * BS) & (qpos < (t + 1) * BS) & (kpos <= qpos - t * BS), 0.0, NEG_INF)
                for t in range(q_blocks)]
    own_tiles = []
    for hh in all_heads:
        tiles = []
        for t in range(q_blocks):
            bias = own_base[t]
            if t + 1 < q_blocks:
                later = jnp.broadcast_to(row_bias(sels[hh], first_own + t), (BS, BQ))
                bias = jnp.where(qpos >= (t + 1) * BS, later, bias)
            tiles.append(ss_own[hh][t * BS:(t + 1) * BS] + bias)
        own_tiles.append(tiles)
    ms = [functools.reduce(jnp.maximum, [jnp.max(s, axis=0, keepdims=True) for s in tiles])
          for tiles in own_tiles]
    ps = [[jnp.exp2(s - ms[hh]) for s in own_tiles[hh]] for hh in all_heads]
    l = [functools.reduce(jnp.add, [jnp.sum(p, axis=0, keepdims=True) for p in ps[hh]]) for hh in all_heads]
    m_a, alpha_a = score_finish(0, ss_0, sa_scr, ms)
    for hh in all_heads:
        acc_scr[hh] = value_dot(hh, own_start, BQ, jnp.concatenate([p.astype(BF16) for p in ps[hh]], axis=0))

    def pair(k, carry):
        m_a, alpha_a, l = carry
        m_b, alpha_b = score_stage(2 * k + 1, sb_scr, m_a)
        l = value_stage(2 * k, sa_scr, m_a, alpha_a, l)
        m_n, alpha_n = score_stage(2 * k + 2, sa_scr, m_b)
        l = value_stage(2 * k + 1, sb_scr, m_b, alpha_b, l)
        return m_n, alpha_n, l

    m_a, alpha_a, l = lax.fori_loop(0, n_pairs - 1, pair, (m_a, alpha_a, l))
    last = 2 * (n_pairs - 1)
    m_b, alpha_b = score_stage(last + 1, sb_scr, m_a)
    l = value_stage(last, sa_scr, m_a, alpha_a, l)
    l = value_stage(last + 1, sb_scr, m_b, alpha_b, l)
    for hh in all_heads:
        o_ref[:, hh * HEAD_DIM:(hh + 1) * HEAD_DIM] = (acc_scr[hh] * (1.0 / l[hh])).T.astype(o_ref.dtype)


def _moba_attention(z, vt, batch, seq_len):
    M = z.shape[1]
    BS = MOBA_BLOCK
    n_blocks = seq_len // BS
    H = ATTN_HEADS
    heads = 4
    group = 2
    q_blocks = 2
    assert seq_len % BS == 0 and n_blocks - 1 >= MOBA_TOPK
    assert n_blocks % (2 * group) == 0 and n_blocks % q_blocks == 0
    hb = H // heads
    steps = n_blocks // q_blocks
    BQ = q_blocks * BS
    kern = functools.partial(_moba_kernel, n_blocks=n_blocks, group=group, heads=heads, q_blocks=q_blocks)
    return pl.pallas_call(
        kern,
        grid=(batch * hb, steps),
        in_specs=[
            pl.BlockSpec((heads, BQ, HEAD_DIM), lambda bh, s: (Z_AQ * hb + bh % hb, (bh // hb) * steps + s, 0)),
            pl.BlockSpec((heads, seq_len, HEAD_DIM), lambda bh, s: (Z_AK * hb + bh % hb, bh // hb, 0)),
            pl.BlockSpec((heads, HEAD_DIM, seq_len), lambda bh, s: (bh % hb, 0, bh // hb)),
        ],
        out_specs=pl.BlockSpec((BQ, heads * HEAD_DIM), lambda bh, s: ((bh // hb) * steps + s, bh % hb)),
        out_shape=jax.ShapeDtypeStruct((M, ATTN_WIDTH), BF16),
        scratch_shapes=[pltpu.VMEM((heads, n_blocks, HEAD_DIM), F32),
                        pltpu.VMEM((heads, group * BS, BQ), F32),
                        pltpu.VMEM((heads, group * BS, BQ), F32),
                        pltpu.VMEM((heads, HEAD_DIM, BQ), F32)],
        compiler_params=pltpu.CompilerParams(
            dimension_semantics=("parallel", "arbitrary"), vmem_limit_bytes=VMEM_LIMIT_BYTES),
        name="moba_attention",
    )(z, z, vt)


def _retention_kernel(lg_ref, q_ref, k_ref, v_ref, g_ref, gret_ref, o_ref, s_scr, *, chunk, n_chunks):
    C = chunk
    head = pl.program_id(0) % RET_HEADS
    lg = lg_ref[head]

    @pl.when(pl.program_id(1) == 0)
    def _():
        s_scr[...] = jnp.zeros_like(s_scr)

    row = lax.broadcasted_iota(jnp.int32, (C, C), 0)
    col = lax.broadcasted_iota(jnp.int32, (C, C), 1)
    rel = (row - col).astype(F32)
    decay = jnp.where(rel >= 0, jnp.exp(lg * jnp.maximum(rel, 0.0)), 0.0)
    pos = lax.broadcasted_iota(jnp.int32, (C, HEAD_DIM), 0).astype(F32)
    k_decay = jnp.exp(lg * (C - 1.0 - pos))
    q_decay = jnp.exp(lg * (pos + 1.0))
    chunk_decay = jnp.exp(jnp.full((1, HEAD_DIM), lg * C, F32))
    gret = gret_ref[...]

    S = s_scr[...]
    for c in range(n_chunks):
        sl = slice(c * C, (c + 1) * C)
        q = q_ref[0, sl, :]
        k = k_ref[0, sl, :]
        v = v_ref[0, sl, :]
        s = lax.dot_general(q, k, _NT, preferred_element_type=F32) * decay
        intra = jnp.dot(s.astype(BF16), v, preferred_element_type=F32)
        cross = jnp.dot((q.astype(F32) * q_decay).astype(BF16), S.astype(BF16),
                        preferred_element_type=F32)
        r = intra + cross
        U = lax.dot_general((k.astype(F32) * k_decay).astype(BF16), v, _TN, preferred_element_type=F32)
        S = chunk_decay * S + U
        r = r * _rms_scale(r) * gret
        gt = g_ref[0, sl, :].astype(F32)
        o_ref[sl, :] = (gt * _sigmoid(gt) * r).astype(o_ref.dtype)
    s_scr[...] = S


def _retention(z, log_decay, g_ret, batch, seq_len):
    M = z.shape[1]
    H = RET_HEADS
    chunk = _pick(seq_len, 256)
    tb = _pick(seq_len, 4096)
    t_blocks = seq_len // tb

    def zspec(seg):
        return pl.BlockSpec((1, tb, HEAD_DIM), lambda bh, t: (seg * H + bh % H, (bh // H) * t_blocks + t, 0))

    kern = functools.partial(_retention_kernel, chunk=chunk, n_chunks=tb // chunk)
    return pl.pallas_call(
        kern,
        grid=(batch * H, t_blocks),
        in_specs=[
            pl.BlockSpec(memory_space=pltpu.SMEM),
            zspec(Z_RQ), zspec(Z_RK), zspec(Z_RV), zspec(Z_RG),
            pl.BlockSpec((1, HEAD_DIM), lambda bh, t: (0, bh % H)),
        ],
        out_specs=pl.BlockSpec((tb, HEAD_DIM), lambda bh, t: ((bh // H) * t_blocks + t, bh % H)),
        out_shape=jax.ShapeDtypeStruct((M, RET_WIDTH), BF16),
        scratch_shapes=[pltpu.VMEM((HEAD_DIM, HEAD_DIM), F32)],
        compiler_params=pltpu.CompilerParams(
            dimension_semantics=("parallel", "arbitrary"), vmem_limit_bytes=VMEM_LIMIT_BYTES),
        name="retention",
    )(log_decay, z, z, z, z, g_ret)


def _oproj_kernel(a_ref, r_ref, wa_ref, wr_ref, x_ref, g_ref, x1_ref, h_ref):
    mix = (jnp.dot(a_ref[...], wa_ref[...], preferred_element_type=F32)
           + jnp.dot(r_ref[...], wr_ref[...], preferred_element_type=F32))
    x1 = x_ref[...] + mix
    x1_ref[...] = x1
    h_ref[...] = (x1 * _rms_scale(x1) * g_ref[...]).astype(h_ref.dtype)


def _out_projection(a_out, r_out, w_o, x, g_ffn):
    M, D = x.shape
    tm = _pick(M, 512)
    assert ATTN_WIDTH == RET_WIDTH
    return pl.pallas_call(
        _oproj_kernel,
        grid=(M // tm,),
        in_specs=[
            pl.BlockSpec((tm, ATTN_WIDTH), lambda i: (i, 0)),
            pl.BlockSpec((tm, RET_WIDTH), lambda i: (i, 0)),
            pl.BlockSpec((ATTN_WIDTH, D), lambda i: (0, 0)),
            pl.BlockSpec((RET_WIDTH, D), lambda i: (1, 0)),
            pl.BlockSpec((tm, D), lambda i: (i, 0)),
            pl.BlockSpec((1, D), lambda i: (0, 0)),
        ],
        out_specs=[pl.BlockSpec((tm, D), lambda i: (i, 0)), pl.BlockSpec((tm, D), lambda i: (i, 0))],
        out_shape=[jax.ShapeDtypeStruct((M, D), F32), jax.ShapeDtypeStruct((M, D), BF16)],
        compiler_params=pltpu.CompilerParams(
            dimension_semantics=("parallel",), vmem_limit_bytes=VMEM_LIMIT_BYTES),
        name="out_projection",
    )(a_out, r_out, w_o, w_o, x, g_ffn)


def _ffn_kernel(h_ref, wg_ref, wu_ref, wd_ref, y_ref):
    def step(first):
        h = h_ref[...]
        gate = jnp.dot(h, wg_ref[...].astype(BF16), preferred_element_type=F32)
        up = jnp.dot(h, wu_ref[...].astype(BF16), preferred_element_type=F32)
        act = (gate * _sigmoid(gate) * up).astype(BF16)
        part = jnp.dot(act, wd_ref[...].astype(BF16), preferred_element_type=F32)
        if first:
            y_ref[...] = part
        else:
            y_ref[...] += part

    f = pl.program_id(1)
    pl.when(f == 0)(lambda: step(True))
    pl.when(f > 0)(lambda: step(False))


def _ffn(h2, w_gate, w_up, w_down):
    M, D = h2.shape
    F = w_gate.shape[1]
    tm = _pick(M, 1024)
    tf = _pick(F, 512)
    return pl.pallas_call(
        _ffn_kernel,
        grid=(M // tm, F // tf),
        in_specs=[
            pl.BlockSpec((tm, D), lambda i, f: (i, 0)),
            pl.BlockSpec((D, tf), lambda i, f: (0, f)),
            pl.BlockSpec((D, tf), lambda i, f: (0, f)),
            pl.BlockSpec((tf, D), lambda i, f: (f, 0)),
        ],
        out_specs=pl.BlockSpec((tm, D), lambda i, f: (i, 0)),
        out_shape=jax.ShapeDtypeStruct((M, D), F32),
        compiler_params=pltpu.CompilerParams(
            dimension_semantics=("parallel", "arbitrary"), vmem_limit_bytes=VMEM_LIMIT_BYTES),
        name="swiglu_ffn",
    )(h2, w_gate, w_up, w_down)


def _ple_kernel(x1_ref, y_ref, p_ref, g_ref, wg_ref, b_ref, wp_ref, o_ref):
    x2 = x1_ref[...] + y_ref[...]
    h = (x2 * _rms_scale(x2) * g_ref[...]).astype(BF16)
    gate = _sigmoid(jnp.dot(h, wg_ref[...], preferred_element_type=F32) + b_ref[...])
    proj = jnp.dot(p_ref[...].astype(BF16), wp_ref[...], preferred_element_type=F32)
    o_ref[...] = x2 + proj * gate


def _ple(x1, y, p, g_ple, w_gate, b_gate, w_proj):
    M, D = x1.shape
    P = p.shape[1]
    tm = _pick(M, 512)
    return pl.pallas_call(
        _ple_kernel,
        grid=(M // tm,),
        in_specs=[
            pl.BlockSpec((tm, D), lambda i: (i, 0)),
            pl.BlockSpec((tm, D), lambda i: (i, 0)),
            pl.BlockSpec((tm, P), lambda i: (i, 0)),
            pl.BlockSpec((1, D), lambda i: (0, 0)),
            pl.BlockSpec((D, D), lambda i: (0, 0)),
            pl.BlockSpec((1, D), lambda i: (0, 0)),
            pl.BlockSpec((P, D), lambda i: (0, 0)),
        ],
        out_specs=pl.BlockSpec((tm, D), lambda i: (i, 0)),
        out_shape=jax.ShapeDtypeStruct((M, D), F32),
        compiler_params=pltpu.CompilerParams(
            dimension_semantics=("parallel",), vmem_limit_bytes=VMEM_LIMIT_BYTES),
        name="gated_ple",
    )(x1, y, p, g_ple, w_gate, b_gate, w_proj)


def _rope_tables(seq_len):
    pos = np.arange(seq_len, dtype=np.float64)
    inv_attn = np.power(ROPE_THETA, -np.arange(0, HEAD_DIM, 2, dtype=np.float64) / HEAD_DIM)
    inv_ret = np.power(ROPE_THETA, -np.linspace(0.0, 1.0, HEAD_DIM // 2))

    def table(inv_freq):
        ang = pos[:, None] * inv_freq[None, :]
        c, s = np.cos(ang), np.sin(ang)
        return np.concatenate([c, c], axis=-1), np.concatenate([-s, s], axis=-1)

    ca, sa = table(inv_attn)
    cr, sr = table(inv_ret)
    return jnp.asarray(np.stack([ca, cr]), F32), jnp.asarray(np.stack([sa, sr]), F32)


def kernel(x, p, g_mix, w_in, q_norm, k_norm, g_ret, w_o, g_ffn, w_gate, w_up, w_down, g_ple,
           w_ple_gate, b_ple_gate, w_ple_proj):
    B, T, D = x.shape
    depth = w_in.shape[0]
    M = B * T
    assert w_in.shape[2] == 3 * ATTN_WIDTH + 4 * RET_WIDTH and ATTN_WIDTH == RET_WIDTH

    cos_tab, sin_tab = _rope_tables(T)
    log_decay = jnp.log(1.0 - jnp.power(2.0, -5.0 - jnp.arange(RET_HEADS, dtype=F32)))
    ones = jnp.ones((HEAD_DIM,), F32)

    xf = x.reshape(M, D)
    for i in range(depth):
        gains = jnp.stack([q_norm[i] * (LOG2_E * HEAD_DIM ** -0.5), k_norm[i], ones, ones,
                           ones * (HEAD_DIM ** -0.5), ones, ones])
        z, vt = _in_projection(xf, g_mix[i][None], w_in[i].astype(BF16), gains[:, None, :],
                               cos_tab, sin_tab, T)
        a_out = _moba_attention(z, vt, B, T)
        r_out = _retention(z, log_decay, g_ret[i][None], B, T)
        x1, h2 = _out_projection(a_out, r_out, w_o[i].astype(BF16), xf, g_ffn[i][None])
        y = _ffn(h2, w_gate[i], w_up[i], w_down[i])
        xf = _ple(x1, y, p[i].reshape(M, -1), g_ple[i][None], w_ple_gate[i].astype(BF16),
                  b_ple_gate[i][None], w_ple_proj[i].astype(BF16))
    return xf.reshape(B, T, D)
```
